```python
import jax, jax.numpy as jnp
from jax import lax
import numpy as np

D_MODEL = 2048
BATCH = 8
SEQ = 2048
DEPTH = 4

MEM_LEN = 256
BRANCH_W = 1024
N_BRANCH = 4
CHUNK = 128
GMLP_GROUPS = 8
GMLP_GROUP_W = BRANCH_W // GMLP_GROUPS
CONV_W = 31
POOL_WINDOWS = (2, 4, 8, 16)
POOL_GROUPS = 4
POOL_GROUP_W = BRANCH_W // POOL_GROUPS
MEM_HEADS = 4
MEM_HEAD_DIM = BRANCH_W // MEM_HEADS
EPS = 1e-6

SPLIT_SIZES = (BRANCH_W, BRANCH_W, BRANCH_W,
               2 * BRANCH_W, BRANCH_W,
               BRANCH_W, BRANCH_W,
               BRANCH_W, BRANCH_W,
               N_BRANCH * D_MODEL)
IN_COLS = sum(SPLIT_SIZES)

kernel_name = "hybrid_gmlp_conv_pool_memattn_trunk"


def rmsnorm(x, g):
    xf = x.astype(jnp.float32)
    y = xf * lax.rsqrt(jnp.mean(xf * xf, axis=-1, keepdims=True) + EPS)
    return (y * g.astype(jnp.float32)).astype(x.dtype)


def layernorm(x, g, b):
    xf = x.astype(jnp.float32)
    mu = jnp.mean(xf, axis=-1, keepdims=True)
    var = jnp.mean(jnp.square(xf - mu), axis=-1, keepdims=True)
    y = (xf - mu) * lax.rsqrt(var + EPS)
    return (y * g.astype(jnp.float32) + b.astype(jnp.float32)).astype(x.dtype)


def gmlp_spatial(u, v, ln_g, ln_b, w_s, b_s):
    bn, t, w = v.shape
    u = jax.nn.gelu(u)
    v = layernorm(jax.nn.gelu(v), ln_g, ln_b)
    mask = jnp.tril(jnp.ones((CHUNK, CHUNK), dtype=bool))
    ws = jnp.where(mask[None], w_s, 0).astype(v.dtype)
    vc = v.reshape(bn, t // CHUNK, CHUNK, GMLP_GROUPS, GMLP_GROUP_W)
    s = jnp.einsum('gts,bcsgd->bctgd', ws, vc) + b_s.T[:, :, None].astype(v.dtype)
    return u * s.reshape(bn, t, w)


def conformer_conv(z, dw_w, dw_b, ln_g, ln_b):
    a, b = jnp.split(z, 2, axis=-1)
    y = a * jax.nn.sigmoid(b)
    y = lax.conv_general_dilated(
        y, dw_w[:, None, :].astype(y.dtype), window_strides=(1,),
        padding=[(CONV_W - 1, 0)], dimension_numbers=('NWC', 'WIO', 'NWC'),
        feature_group_count=BRANCH_W) + dw_b
    y = layernorm(y, ln_g, ln_b)
    return jax.nn.silu(y)


def multiscale_pool(c, w_pool, scale):
    bn, t, w = c.shape
    cg = c.reshape(bn, t, POOL_GROUPS, POOL_GROUP_W).astype(jnp.float32)
    cs = jnp.cumsum(cg, axis=1)
    t1 = jnp.arange(1, t + 1)
    pooled = []
    for g, win in enumerate(POOL_WINDOWS):
        csg = cs[:, :, g]
        lag = jnp.pad(csg, ((0, 0), (win, 0), (0, 0)))[:, :t]
        cnt = jnp.minimum(t1, win).astype(jnp.float32)[None, :, None]
        pooled.append((csg - lag) / cnt)
    pooled = jnp.stack(pooled, axis=2)
    diff = (pooled - cg).astype(c.dtype)
    y = jnp.einsum('btgc,gcd->btgd', diff, w_pool)
    return y.reshape(bn, t, w) * scale


def memory_attention(q, mem_n, w_kv):
    bn, t, w = q.shape
    kv = jnp.einsum('bmd,dk->bmk', mem_n, w_kv)
    k, v = jnp.split(kv, 2, axis=-1)
    qh = q.reshape(bn, t, MEM_HEADS, MEM_HEAD_DIM) * (MEM_HEAD_DIM ** -0.5)
    kh = k.reshape(bn, -1, MEM_HEADS, MEM_HEAD_DIM)
    vh = v.reshape(bn, -1, MEM_HEADS, MEM_HEAD_DIM)
    s = jnp.einsum('bthd,bmhd->bhtm', qh, kh).astype(jnp.float32)
    p = jax.nn.softmax(s, axis=-1).astype(vh.dtype)
    o = jnp.einsum('bhtm,bmhd->bthd', p, vh)
    return o.reshape(bn, t, w)


def setup_inputs(seed: int = 0) -> dict:
    key = jax.random.key(seed)
    ks = jax.random.split(key, 20)
    f32 = jnp.float32
    nrm = lambda k, shp, sc: jax.random.normal(k, shp, f32) * sc
    L, D, W = DEPTH, D_MODEL, BRANCH_W
    return {
        "x": jax.random.normal(ks[0], (BATCH, SEQ, D), f32),
        "mem": jax.random.normal(ks[1], (BATCH, MEM_LEN, D), f32),
        "norm_g": 1.0 + nrm(ks[2], (L, D), 0.02),
        "w_in": nrm(ks[3], (L, D, IN_COLS), D ** -0.5),
        "gmlp_ln_g": 1.0 + nrm(ks[4], (L, W), 0.02),
        "gmlp_ln_b": nrm(ks[5], (L, W), 0.02),
        "gmlp_ws": nrm(ks[6], (L, GMLP_GROUPS, CHUNK, CHUNK), CHUNK ** -0.5),
        "gmlp_bs": 1.0 + nrm(ks[7], (L, GMLP_GROUPS, CHUNK), 0.02),
        "conv_w": nrm(ks[8], (L, CONV_W, W), CONV_W ** -0.5),
        "conv_b": nrm(ks[9], (L, W), 0.02),
        "conv_ln_g": 1.0 + nrm(ks[10], (L, W), 0.02),
        "conv_ln_b": nrm(ks[11], (L, W), 0.02),
        "pool_w": nrm(ks[12], (L, POOL_GROUPS, POOL_GROUP_W, POOL_GROUP_W), POOL_GROUP_W ** -0.5),
        "pool_scale": 1.0 + nrm(ks[13], (L, W), 0.02),
        "mem_norm_g": 1.0 + nrm(ks[14], (L, D), 0.02),
        "w_kv": nrm(ks[15], (L, D, 2 * W), D ** -0.5),
        "w_branch": nrm(ks[16], (L, N_BRANCH, W, D), W ** -0.5),
        "w_out": nrm(ks[17], (L, D, D), D ** -0.5),
        "final_g": 1.0 + nrm(ks[18], (D,), 0.02),
    }


def reference(x, mem, norm_g, w_in, gmlp_ln_g, gmlp_ln_b, gmlp_ws, gmlp_bs, conv_w, conv_b,
              conv_ln_g, conv_ln_b, pool_w, pool_scale, mem_norm_g, w_kv, w_branch, w_out, final_g):
    bn, t, d = x.shape
    split_points = [int(p) for p in np.cumsum(SPLIT_SIZES)[:-1]]
    for l in range(DEPTH):
        h = rmsnorm(x, norm_g[l])
        z = jnp.einsum('btd,dk->btk', h, w_in[l])
        (a_u, a_v, a_g, b_in, b_g, c_in, c_g, m_q, m_g, gates) = jnp.split(z, split_points, axis=-1)
        y_a = gmlp_spatial(a_u, a_v, gmlp_ln_g[l], gmlp_ln_b[l], gmlp_ws[l], gmlp_bs[l]) * jax.nn.silu(a_g)
        y_b = conformer_conv(b_in, conv_w[l], conv_b[l], conv_ln_g[l], conv_ln_b[l]) * jax.nn.silu(b_g)
        y_c = multiscale_pool(c_in, pool_w[l], pool_scale[l]) * jax.nn.silu(c_g)
        mem_n = rmsnorm(mem, mem_norm_g[l])
        y_m = memory_attention(m_q, mem_n, w_kv[l]) * jax.nn.silu(m_g)
        ys = jnp.stack([y_a, y_b, y_c, y_m], axis=2)
        proj = jnp.einsum('btnw,nwd->btnd', ys, w_branch[l])
        g = jax.nn.sigmoid(gates.reshape(bn, t, N_BRANCH, d))
        merged = jnp.sum(g * proj, axis=2)
        x = x + jnp.einsum('btd,de->bte', merged, w_out[l])
    return rmsnorm(x, final_g)
```

```python
import functools

import jax
import jax.numpy as jnp
from jax import lax
from jax.experimental import pallas as pl
from jax.experimental.pallas import tpu as pltpu

F32 = jnp.float32
BF16 = jnp.bfloat16

D = 2048
W = 1024
CHUNK = 128
G_GMLP = 8
CONV_W = 31
POOL_WINDOWS = (2, 4, 8, 16)
POOL_GW = 256
HEADS = 4
HEAD_DIM = 256
EPS = 1e-6
N_BRANCH = 4
LANE = 128

SUBLANES = 8
PAD = 32
SHIFT_ROWS = 128
TR = 256
GATE_COL0 = 10 * W

MIB = 1024 * 1024


def _params(sem, vmem_mib):
    return pltpu.CompilerParams(dimension_semantics=sem, vmem_limit_bytes=vmem_mib * MIB)


def _layernorm(x, g, b):
    mu = jnp.mean(x, axis=-1, keepdims=True)
    xc = x - mu
    var = jnp.mean(xc * xc, axis=-1, keepdims=True)
    return xc * lax.rsqrt(var + EPS) * g + b


def _rms(x, g):
    return x * lax.rsqrt(jnp.mean(x * x, axis=-1, keepdims=True) + EPS) * g


def _silu(x):
    return x * jax.nn.sigmoid(x)


def _row_tiles(t_len, body):
    def step(r, carry):
        body(pl.multiple_of(r * TR, TR))
        return carry
    lax.fori_loop(0, t_len // TR, step, 0)


def _rmsnorm_kernel(x_ref, g_ref, o_ref):
    o_ref[...] = _rms(x_ref[...], g_ref[...]).astype(o_ref.dtype)


def _rmsnorm(x2d, g, out_dtype, tm=512):
    n = x2d.shape[0]
    return pl.pallas_call(
        _rmsnorm_kernel,
        grid=(n // tm,),
        in_specs=[pl.BlockSpec((tm, D), lambda i: (i, 0)),
                  pl.BlockSpec((1, D), lambda i: (0, 0))],
        out_specs=pl.BlockSpec((tm, D), lambda i: (i, 0)),
        out_shape=jax.ShapeDtypeStruct((n, D), out_dtype),
        compiler_params=_params(("arbitrary",), 40),
        name="rmsnorm",
    )(x2d, g.reshape(1, D))


def _kv_kernel(m_ref, g_ref, w_ref, o_ref):
    mn = _rms(m_ref[...], g_ref[...]).astype(BF16)
    o_ref[...] = jnp.dot(mn, w_ref[...], preferred_element_type=F32).astype(BF16)


def _memory_kv(mem2d, mem_norm_g, w_kv_bf, tm=512):
    n_layers = w_kv_bf.shape[0]
    n = mem2d.shape[0]
    tm = min(tm, n)
    return pl.pallas_call(
        _kv_kernel,
        grid=(n_layers, n // tm),
        in_specs=[pl.BlockSpec((tm, D), lambda l, i: (i, 0)),
                  pl.BlockSpec((None, 1, D), lambda l, i: (l, 0, 0)),
                  pl.BlockSpec((None, D, 2 * W), lambda l, i: (l, 0, 0))],
        out_specs=pl.BlockSpec((None, tm, 2 * W), lambda l, i: (l, i, 0)),
        out_shape=jax.ShapeDtypeStruct((n_layers, n, 2 * W), BF16),
        compiler_params=_params(("arbitrary", "arbitrary"), 48),
        name="memory_kv",
    )(mem2d, mem_norm_g.reshape(n_layers, 1, D), w_kv_bf)


def _branch_a_kernel(h_ref, w_ref, lng_ref, lnb_ref, ws_ref, bias_ref, o_ref,
                     sc_ref, wsm_ref, vln_ref):
    s = pl.program_id(1)
    t_len = h_ref.shape[0]

    def proj(r0):
        return jnp.dot(h_ref[pl.ds(r0, TR), :], w_ref[...], preferred_element_type=F32)

    @pl.when(s == 0)
    def _():
        row = lax.broadcasted_iota(jnp.int32, (CHUNK, CHUNK), 0)
        col = lax.broadcasted_iota(jnp.int32, (CHUNK, CHUNK), 1)
        for g in range(G_GMLP):
            wsm_ref[g] = jnp.where(row >= col, ws_ref[g], 0.0).astype(BF16)

        def body(r0):
            sc_ref[pl.ds(r0, TR), :] = jax.nn.gelu(proj(r0))
        _row_tiles(t_len, body)

    @pl.when(s == 1)
    def _():
        def body(r0):
            v = _layernorm(jax.nn.gelu(proj(r0)), lng_ref[...], lnb_ref[...])
            vln_ref[...] = v.astype(BF16)
            for c in range(TR // CHUNK):
                for g in range(G_GMLP):
                    rows = pl.ds(r0 + c * CHUNK, CHUNK)
                    cols = pl.ds(g * LANE, LANE)
                    sp = jnp.dot(wsm_ref[g], vln_ref[pl.ds(c * CHUNK, CHUNK), cols],
                                 preferred_element_type=F32) + bias_ref[:, cols]
                    sc_ref[rows, cols] = sc_ref[rows, cols] * sp
        _row_tiles(t_len, body)

    @pl.when(s == 2)
    def _():
        def body(r0):
            rows = pl.ds(r0, TR)
            o_ref[rows, :] = (sc_ref[rows, :] * _silu(proj(r0))).astype(BF16)
        _row_tiles(t_len, body)


def _branch_a(h3, w_in_bf, l, ln_g, ln_b, ws, bias_full):
    b_sz, t_len, _ = h3.shape
    return pl.pallas_call(
        _branch_a_kernel,
        grid=(b_sz, 3),
        in_specs=[pl.BlockSpec((None, t_len, D), lambda b, s: (b, 0, 0)),
                  pl.BlockSpec((None, D, W), lambda b, s: (l, 0, s)),
                  pl.BlockSpec((1, W), lambda b, s: (0, 0)),
                  pl.BlockSpec((1, W), lambda b, s: (0, 0)),
                  pl.BlockSpec((G_GMLP, CHUNK, CHUNK), lambda b, s: (0, 0, 0)),
                  pl.BlockSpec((CHUNK, W), lambda b, s: (0, 0))],
        out_specs=pl.BlockSpec((None, t_len, W), lambda b, s: (b, 0, 0)),
        out_shape=jax.ShapeDtypeStruct((b_sz, t_len, W), BF16),
        scratch_shapes=[pltpu.VMEM((t_len, W), F32),
                        pltpu.VMEM((G_GMLP, CHUNK, CHUNK), BF16),
                        pltpu.VMEM((TR, W), BF16)],
        compiler_params=_params(("arbitrary", "arbitrary"), 56),
        name="branch_a",
    )(h3, w_in_bf, ln_g.reshape(1, W), ln_b.reshape(1, W), ws, bias_full)


def _branch_b_kernel(h_ref, w_ref, cw_ref, cb_ref, lng_ref, lnb_ref, o_ref, sc_ref, cv_ref):
    s = pl.program_id(1)
    t_len = h_ref.shape[0]

    def proj(r0):
        return jnp.dot(h_ref[pl.ds(r0, TR), :], w_ref[...], preferred_element_type=F32)

    @pl.when(s == 0)
    def _():
        sc_ref[pl.ds(0, PAD), :] = jnp.zeros((PAD, W), F32)

        def body(r0):
            sc_ref[pl.ds(PAD + r0, TR), :] = proj(r0)
        _row_tiles(t_len, body)

    @pl.when(s == 1)
    def _():
        def body(r0):
            rows = pl.ds(PAD + r0, TR)
            sc_ref[rows, :] = sc_ref[rows, :] * jax.nn.sigmoid(proj(r0))
        _row_tiles(t_len, body)

    @pl.when(s == 2)
    def _():
        def body(r0):
            for half in range(TR // SHIFT_ROWS):
                for j in range(W // LANE):
                    cols = pl.ds(j * LANE, LANE)
                    win = sc_ref[pl.ds(r0 + half * SHIFT_ROWS, SHIFT_ROWS + PAD), cols]
                    acc = None
                    for sub in range(SUBLANES):
                        rolled = win if sub == 0 else pltpu.roll(win, sub, 0)
                        for k in range(CONV_W):
                            shift = CONV_W - 1 - k
                            if shift % SUBLANES != sub:
                                continue
                            lo = PAD - (shift - sub)
                            term = rolled[lo:lo + SHIFT_ROWS] * cw_ref[k:k + 1, cols]
                            acc = term if acc is None else acc + term
                    cv_ref[pl.ds(half * SHIFT_ROWS, SHIFT_ROWS), cols] = acc + cb_ref[:, cols]
            y = _silu(_layernorm(cv_ref[...], lng_ref[...], lnb_ref[...]))
            o_ref[pl.ds(r0, TR), :] = (y * _silu(proj(r0))).astype(BF16)
        _row_tiles(t_len, body)


def _branch_b(h3, w_in_bf, l, conv_w, conv_b, ln_g, ln_b):
    b_sz, t_len, _ = h3.shape
    return pl.pallas_call(
        _branch_b_kernel,
        grid=(b_sz, 3),
        in_specs=[pl.BlockSpec((None, t_len, D), lambda b, s: (b, 0, 0)),
                  pl.BlockSpec((None, D, W), lambda b, s: (l, 0, 3 + s)),
                  pl.BlockSpec((CONV_W, W), lambda b, s: (0, 0)),
                  pl.BlockSpec((1, W), lambda b, s: (0, 0)),
                  pl.BlockSpec((1, W), lambda b, s: (0, 0)),
                  pl.BlockSpec((1, W), lambda b, s: (0, 0))],
        out_specs=pl.BlockSpec((None, t_len, W), lambda b, s: (b, 0, 0)),
        out_shape=jax.ShapeDtypeStruct((b_sz, t_len, W), BF16),
        scratch_shapes=[pltpu.VMEM((PAD + t_len, W), F32),
                        pltpu.VMEM((TR, W), F32)],
        compiler_params=_params(("arbitrary", "arbitrary"), 56),
        name="branch_b",
    )(h3, w_in_bf, conv_w, conv_b.reshape(1, W), ln_g.reshape(1, W), ln_b.reshape(1, W))


def _branch_c_kernel(h_ref, w_ref, pw_ref, scale_ref, o_ref, sc_ref, yc_ref):
    s = pl.program_id(1)
    t_len = h_ref.shape[0]

    def proj(r0):
        return jnp.dot(h_ref[pl.ds(r0, TR), :], w_ref[...], preferred_element_type=F32)

    @pl.when(s == 0)
    def _():
        sc_ref[pl.ds(0, PAD), :] = jnp.zeros((PAD, W), F32)

        def fill(r0):
            sc_ref[pl.ds(PAD + r0, TR), :] = proj(r0)
        _row_tiles(t_len, fill)

        def pool(r0):
            for half in range(TR // SHIFT_ROWS):
                t0 = r0 + half * SHIFT_ROWS
                t1 = (t0 + 1 + lax.broadcasted_iota(jnp.int32, (SHIFT_ROWS, POOL_GW), 0)).astype(F32)
                for g, win in enumerate(POOL_WINDOWS):
                    cols = pl.ds(g * POOL_GW, POOL_GW)
                    wsum = sc_ref[pl.ds(t0, SHIFT_ROWS + PAD), cols]
                    cur = wsum[PAD:]
                    span = 1
                    while span < win:
                        wsum = wsum + pltpu.roll(wsum, span, 0)
                        span *= 2
                    diff = wsum[PAD:] / jnp.minimum(t1, float(win)) - cur
                    y = jnp.dot(diff.astype(BF16), pw_ref[g].astype(BF16), preferred_element_type=F32)
                    yc_ref[pl.ds(t0, SHIFT_ROWS), cols] = y * scale_ref[:, cols]
        _row_tiles(t_len, pool)

    @pl.when(s == 1)
    def _():
        def body(r0):
            rows = pl.ds(r0, TR)
            o_ref[rows, :] = (yc_ref[rows, :] * _silu(proj(r0))).astype(BF16)
        _row_tiles(t_len, body)


def _branch_c(h3, w_in_bf, l, pool_w, pool_scale):
    b_sz, t_len, _ = h3.shape
    n_groups = len(POOL_WINDOWS)
    return pl.pallas_call(
        _branch_c_kernel,
        grid=(b_sz, 2),
        in_specs=[pl.BlockSpec((None, t_len, D), lambda b, s: (b, 0, 0)),
                  pl.BlockSpec((None, D, W), lambda b, s: (l, 0, 6 + s)),
                  pl.BlockSpec((n_groups, POOL_GW, POOL_GW), lambda b, s: (0, 0, 0)),
                  pl.BlockSpec((1, W), lambda b, s: (0, 0))],
        out_specs=pl.BlockSpec((None, t_len, W), lambda b, s: (b, 0, 0)),
        out_shape=jax.ShapeDtypeStruct((b_sz, t_len, W), BF16),
        scratch_shapes=[pltpu.VMEM((PAD + t_len, W), F32),
                        pltpu.VMEM((t_len, W), F32)],
        compiler_params=_params(("arbitrary", "arbitrary"), 56),
        name="branch_c",
    )(h3, w_in_bf, pool_w, pool_scale.reshape(1, W))


def _branch_m_kernel(h_ref, w_ref, kv_ref, o_ref, sc_ref):
    s = pl.program_id(1)
    t_len = h_ref.shape[0]

    def proj(r0):
        return jnp.dot(h_ref[pl.ds(r0, TR), :], w_ref[...], preferred_element_type=F32)

    @pl.when(s == 0)
    def _():
        def body(r0):
            q = (proj(r0) * (HEAD_DIM ** -0.5)).astype(BF16)
            for hd in range(HEADS):
                cols = pl.ds(hd * HEAD_DIM, HEAD_DIM)
                sc = lax.dot_general(q[:, hd * HEAD_DIM:(hd + 1) * HEAD_DIM], kv_ref[:, cols],
                                     (((1,), (1,)), ((), ())), preferred_element_type=F32)
                e = jnp.exp(sc - jnp.max(sc, axis=-1, keepdims=True))
                p = (e / jnp.sum(e, axis=-1, keepdims=True)).astype(BF16)
                sc_ref[pl.ds(r0, TR), cols] = jnp.dot(
                    p, kv_ref[:, pl.ds(W + hd * HEAD_DIM, HEAD_DIM)], preferred_element_type=F32)
        _row_tiles(t_len, body)

    @pl.when(s == 1)
    def _():
        def body(r0):
            rows = pl.ds(r0, TR)
            o_ref[rows, :] = (sc_ref[rows, :] * _silu(proj(r0))).astype(BF16)
        _row_tiles(t_len, body)


def _branch_m(h3, w_in_bf, l, kv, mem_len):
    b_sz, t_len, _ = h3.shape
    return pl.pallas_call(
        _branch_m_kernel,
        grid=(b_sz, 2),
        in_specs=[pl.BlockSpec((None, t_len, D), lambda b, s: (b, 0, 0)),
                  pl.BlockSpec((None, D, W), lambda b, s: (l, 0, 8 + s)),
                  pl.BlockSpec((None, mem_len, 2 * W), lambda b, s: (l, b, 0))],
        out_specs=pl.BlockSpec((None, t_len, W), lambda b, s: (b, 0, 0)),
        out_shape=jax.ShapeDtypeStruct((b_sz, t_len, W), BF16),
        scratch_shapes=[pltpu.VMEM((t_len, W), F32)],
        compiler_params=_params(("arbitrary", "arbitrary"), 56),
        name="branch_m",
    )(h3, w_in_bf, kv)


def _merge_kernel(h_ref, ya_ref, yb_ref, yc_ref, ym_ref,
                  wg0, wg1, wg2, wg3, wb0, wb1, wb2, wb3, o_ref):
    h = h_ref[...]
    acc = None
    for y_ref, wg, wb in ((ya_ref, wg0, wb0), (yb_ref, wg1, wb1),
                          (yc_ref, wg2, wb2), (ym_ref, wg3, wb3)):
        gate = jax.nn.sigmoid(jnp.dot(h, wg[...], preferred_element_type=F32))
        term = gate * jnp.dot(y_ref[...], wb[...], preferred_element_type=F32)
        acc = term if acc is None else acc + term
    o_ref[...] = acc.astype(BF16)


def _merge(h2, ys, w_in_bf, w_branch_bf, l, tm=1024, cn=256):
    n = h2.shape[0]
    tm = min(tm, n)
    gate_blk0 = GATE_COL0 // cn
    blk_per_branch = D // cn

    def wg_spec(nb):
        return pl.BlockSpec((None, D, cn), lambda i, c: (l, 0, gate_blk0 + nb * blk_per_branch + c))

    def wb_spec(nb):
        return pl.BlockSpec((None, None, W, cn), lambda i, c: (l, nb, 0, c))

    y_spec = pl.BlockSpec((tm, W), lambda i, c: (i, 0))
    return pl.pallas_call(
        _merge_kernel,
        grid=(n // tm, D // cn),
        in_specs=[pl.BlockSpec((tm, D), lambda i, c: (i, 0))] + [y_spec] * N_BRANCH
                 + [wg_spec(nb) for nb in range(N_BRANCH)]
                 + [wb_spec(nb) for nb in range(N_BRANCH)],
        out_specs=pl.BlockSpec((tm, cn), lambda i, c: (i, c)),
        out_shape=jax.ShapeDtypeStruct((n, D), BF16),
        compiler_params=_params(("arbitrary", "arbitrary"), 56),
        name="gate_merge",
    )(h2, *ys, *([w_in_bf] * N_BRANCH), *([w_branch_bf] * N_BRANCH))


def _outproj_kernel(x_ref, m_ref, w_ref, g_ref, *out_refs, sub, last):
    def body(r, carry):
        rows = pl.ds(pl.multiple_of(r * sub, sub), sub)
        xn = x_ref[rows, :] + jnp.dot(m_ref[rows, :], w_ref[...], preferred_element_type=F32)
        normed = _rms(xn, g_ref[...])
        if last:
            out_refs[0][rows, :] = normed
        else:
            out_refs[0][rows, :] = xn
            out_refs[1][rows, :] = normed.astype(BF16)
        return carry
    lax.fori_loop(0, x_ref.shape[0] // sub, body, 0)


def _outproj(x2, merged, w_out_bf, l, g_next, last, tm=512, sub=256):
    n = x2.shape[0]
    tm = min(tm, n)
    row_spec = pl.BlockSpec((tm, D), lambda i: (i, 0))
    if last:
        out_specs, out_shape = row_spec, jax.ShapeDtypeStruct((n, D), F32)
    else:
        out_specs = [row_spec, row_spec]
        out_shape = [jax.ShapeDtypeStruct((n, D), F32), jax.ShapeDtypeStruct((n, D), BF16)]
    return pl.pallas_call(
        functools.partial(_outproj_kernel, sub=min(sub, tm), last=last),
        grid=(n // tm,),
        in_specs=[row_spec, row_spec,
                  pl.BlockSpec((None, D, D), lambda i: (l, 0, 0)),
                  pl.BlockSpec((1, D), lambda i: (0, 0))],
        out_specs=out_specs,
        out_shape=out_shape,
        compiler_params=_params(("arbitrary",), 56),
        name="out_proj",
    )(x2, merged, w_out_bf, g_next.reshape(1, D))


def kernel(x, mem, norm_g, w_in, gmlp_ln_g, gmlp_ln_b, gmlp_ws, gmlp_bs, conv_w, conv_b,
           conv_ln_g, conv_ln_b, pool_w, pool_scale, mem_norm_g, w_kv, w_branch, w_out, final_g):
    b_sz, t_len, d = x.shape
    n_layers = w_in.shape[0]
    mem_len = mem.shape[1]
    assert d == D and w_in.shape[2] == GATE_COL0 + N_BRANCH * D
    assert t_len % TR == 0 and TR % CHUNK == 0

    w_in_bf = w_in.astype(BF16)
    w_kv_bf = w_kv.astype(BF16)
    w_branch_bf = w_branch.astype(BF16)
    w_out_bf = w_out.astype(BF16)

    n = b_sz * t_len
    x2 = x.reshape(n, D)
    kv = _memory_kv(mem.reshape(b_sz * mem_len, D), mem_norm_g, w_kv_bf)
    h2 = _rmsnorm(x2, norm_g[0], BF16)

    for l in range(n_layers):
        h3 = h2.reshape(b_sz, t_len, D)
        bias_full = jnp.repeat(gmlp_bs[l].T, LANE, axis=1)
        ys = (
            _branch_a(h3, w_in_bf, l, gmlp_ln_g[l], gmlp_ln_b[l], gmlp_ws[l], bias_full),
            _branch_b(h3, w_in_bf, l, conv_w[l], conv_b[l], conv_ln_g[l], conv_ln_b[l]),
            _branch_c(h3, w_in_bf, l, pool_w[l], pool_scale[l]),
            _branch_m(h3, w_in_bf, l, kv, mem_len),
        )
        ys = tuple(y.reshape(n, W) for y in ys)
        merged = _merge(h2, ys, w_in_bf, w_branch_bf, l)
        last = l == n_layers - 1
        g_next = final_g if last else norm_g[l + 1]
        res = _outproj(x2, merged, w_out_bf, l, g_next, last)
        if last:
            return res.reshape(b_sz, t_len, D)
        x2, h2 = res
```

```python
import functools

import jax
import jax.numpy as jnp
from jax import lax
from jax.experimental import pallas as pl
from jax.experimental.pallas import tpu as pltpu

F32 = jnp.float32
BF16 = jnp.bfloat16

D = 2048
W = 1024
CHUNK = 128
G_GMLP = 8
CONV_W = 31
POOL_WINDOWS = (2, 4, 8, 16)
POOL_GW = 256
HEADS = 4
HEAD_DIM = 256
EPS = 1e-6
N_BRANCH = 4
LANE = 128

PAD = 32
SHIFT_ROWS = 128
TR = 256
GATE_COL0 = 10 * W

MIB = 1024 * 1024


def _params(sem, vmem_mib):
    return pltpu.CompilerParams(dimension_semantics=sem, vmem_limit_bytes=vmem_mib * MIB)


def _layernorm(x, g, b):
    mu = jnp.mean(x, axis=-1, keepdims=True)
    xc = x - mu
    var = jnp.mean(xc * xc, axis=-1, keepdims=True)
    return xc * lax.rsqrt(var + EPS) * g + b


def _rms(x, g):
    return x * lax.rsqrt(jnp.mean(x * x, axis=-1, keepdims=True) + EPS) * g


def _silu(x):
    return x * jax.nn.sigmoid(x)


def _row_tiles(h_ref, w_ref, epilogue):
    def step(r, carry):
        r0 = pl.multiple_of(r * TR, TR)
        epilogue(r0, jnp.dot(h_ref[pl.ds(r0, TR), :], w_ref[...], preferred_element_type=F32))
        return carry
    lax.fori_loop(0, h_ref.shape[0] // TR, step, 0, unroll=2)


def _pipelined_row_tiles(h_ref, w_ref, z_refs, epilogue):
    t_len = h_ref.shape[0]
    n_tiles = t_len // TR
    z0_ref, z1_ref = z_refs

    def project(r0, z_ref):
        z_ref[...] = jnp.dot(h_ref[pl.ds(r0, TR), :], w_ref[...], preferred_element_type=F32)

    project(0, z0_ref)

    def step(i, carry):
        r0 = pl.multiple_of(i * (2 * TR), 2 * TR)
        project(r0 + TR, z1_ref)
        epilogue(r0, z0_ref)
        project(r0 + 2 * TR, z0_ref)
        epilogue(r0 + TR, z1_ref)
        return carry
    lax.fori_loop(0, n_tiles // 2 - 1, step, 0)
    r0 = t_len - 2 * TR
    project(r0 + TR, z1_ref)
    epilogue(r0, z0_ref)
    epilogue(r0 + TR, z1_ref)


Z_SCRATCH = [pltpu.VMEM((TR, W), F32), pltpu.VMEM((TR, W), F32)]
N_SLAB = W // LANE


def _slab_store(ref, start, value):
    for j in range(N_SLAB):
        ref[j, pl.ds(start, value.shape[0]), :] = value[:, j * LANE:(j + 1) * LANE]


def _shifted_rows(ref, slab, start, n_rows):
    return ref[slab, pl.ds(start, n_rows, stride=1), :]


def _rmsnorm_kernel(x_ref, g_ref, o_ref):
    o_ref[...] = _rms(x_ref[...], g_ref[...]).astype(o_ref.dtype)


def _rmsnorm(x2d, g, out_dtype, tm=512):
    n = x2d.shape[0]
    return pl.pallas_call(
        _rmsnorm_kernel,
        grid=(n // tm,),
        in_specs=[pl.BlockSpec((tm, D), lambda i: (i, 0)),
                  pl.BlockSpec((1, D), lambda i: (0, 0))],
        out_specs=pl.BlockSpec((tm, D), lambda i: (i, 0)),
        out_shape=jax.ShapeDtypeStruct((n, D), out_dtype),
        compiler_params=_params(("arbitrary",), 40),
        name="rmsnorm",
    )(x2d, g.reshape(1, D))


def _kv_kernel(m_ref, g_ref, w_ref, o_ref):
    mn = _rms(m_ref[...], g_ref[...]).astype(BF16)
    o_ref[...] = jnp.dot(mn, w_ref[...], preferred_element_type=F32).astype(BF16)


def _memory_kv(mem2d, mem_norm_g, w_kv_bf, tm=512):
    n_layers = w_kv_bf.shape[0]
    n = mem2d.shape[0]
    tm = min(tm, n)
    return pl.pallas_call(
        _kv_kernel,
        grid=(n_layers, n // tm),
        in_specs=[pl.BlockSpec((tm, D), lambda l, i: (i, 0)),
                  pl.BlockSpec((None, 1, D), lambda l, i: (l, 0, 0)),
                  pl.BlockSpec((None, D, 2 * W), lambda l, i: (l, 0, 0))],
        out_specs=pl.BlockSpec((None, tm, 2 * W), lambda l, i: (l, i, 0)),
        out_shape=jax.ShapeDtypeStruct((n_layers, n, 2 * W), BF16),
        compiler_params=_params(("arbitrary", "arbitrary"), 48),
        name="memory_kv",
    )(mem2d, mem_norm_g.reshape(n_layers, 1, D), w_kv_bf)


def _branch_a_kernel(h_ref, w_ref, lng_ref, lnb_ref, ws_ref, bias_ref, o_ref,
                     sc_ref, wsm_ref, vln_ref, z0_ref, z1_ref):
    s = pl.program_id(1)
    z_refs = (z0_ref, z1_ref)

    @pl.when(s == 0)
    def _():
        row = lax.broadcasted_iota(jnp.int32, (CHUNK, CHUNK), 0)
        col = lax.broadcasted_iota(jnp.int32, (CHUNK, CHUNK), 1)
        for g in range(G_GMLP):
            wsm_ref[g] = jnp.where(row >= col, ws_ref[g], 0.0).astype(BF16)

        def gelu_u(r0, z):
            sc_ref[pl.ds(r0, TR), :] = jax.nn.gelu(z)
        _row_tiles(h_ref, w_ref, gelu_u)

    @pl.when(s == 1)
    def _():
        def spatial_gate(r0, z_ref):
            v = _layernorm(jax.nn.gelu(z_ref[...]), lng_ref[...], lnb_ref[...])
            vln_ref[...] = v.astype(BF16)
            for c in range(TR // CHUNK):
                for g in range(G_GMLP):
                    rows = pl.ds(r0 + c * CHUNK, CHUNK)
                    cols = pl.ds(g * LANE, LANE)
                    sp = jnp.dot(wsm_ref[g], vln_ref[pl.ds(c * CHUNK, CHUNK), cols],
                                 preferred_element_type=F32) + bias_ref[:, cols]
                    sc_ref[rows, cols] = sc_ref[rows, cols] * sp
        _pipelined_row_tiles(h_ref, w_ref, z_refs, spatial_gate)

    @pl.when(s == 2)
    def _():
        def silu_gate(r0, z):
            rows = pl.ds(r0, TR)
            o_ref[rows, :] = (sc_ref[rows, :] * _silu(z)).astype(BF16)
        _row_tiles(h_ref, w_ref, silu_gate)


def _branch_a(h3, w_in_bf, l, ln_g, ln_b, ws, bias_full):
    b_sz, t_len, _ = h3.shape
    return pl.pallas_call(
        _branch_a_kernel,
        grid=(b_sz, 3),
        in_specs=[pl.BlockSpec((None, t_len, D), lambda b, s: (b, 0, 0)),
                  pl.BlockSpec((None, D, W), lambda b, s: (l, 0, s)),
                  pl.BlockSpec((1, W), lambda b, s: (0, 0)),
                  pl.BlockSpec((1, W), lambda b, s: (0, 0)),
                  pl.BlockSpec((G_GMLP, CHUNK, CHUNK), lambda b, s: (0, 0, 0)),
                  pl.BlockSpec((CHUNK, W), lambda b, s: (0, 0))],
        out_specs=pl.BlockSpec((None, t_len, W), lambda b, s: (b, 0, 0)),
        out_shape=jax.ShapeDtypeStruct((b_sz, t_len, W), BF16),
        scratch_shapes=[pltpu.VMEM((t_len, W), F32),
                        pltpu.VMEM((G_GMLP, CHUNK, CHUNK), BF16),
                        pltpu.VMEM((TR, W), BF16)] + Z_SCRATCH,
        compiler_params=_params(("arbitrary", "arbitrary"), 56),
        name="branch_a",
    )(h3, w_in_bf, ln_g.reshape(1, W), ln_b.reshape(1, W), ws, bias_full)


def _branch_b_kernel(h_ref, w_ref, cw_ref, cb_ref, lng_ref, lnb_ref, o_ref,
                     sc_ref, cv_ref, z0_ref, z1_ref):
    s = pl.program_id(1)
    z_refs = (z0_ref, z1_ref)

    @pl.when(s == 0)
    def _():
        sc_ref[:, pl.ds(0, PAD), :] = jnp.zeros((N_SLAB, PAD, LANE), F32)

        def keep(r0, z):
            _slab_store(sc_ref, PAD + r0, z)
        _row_tiles(h_ref, w_ref, keep)

    @pl.when(s == 1)
    def _():
        def glu_conv(r0, z_ref):
            gate = jax.nn.sigmoid(z_ref[...])
            for j in range(N_SLAB):
                rows = pl.ds(PAD + r0, TR)
                sc_ref[j, rows, :] = sc_ref[j, rows, :] * gate[:, j * LANE:(j + 1) * LANE]
            for j in range(N_SLAB):
                cols = pl.ds(j * LANE, LANE)
                for half in range(TR // SHIFT_ROWS):
                    base = r0 + half * SHIFT_ROWS + PAD - (CONV_W - 1)
                    acc = _shifted_rows(sc_ref, j, base, SHIFT_ROWS) * cw_ref[0:1, cols]
                    for k in range(1, CONV_W):
                        acc = acc + _shifted_rows(sc_ref, j, base + k, SHIFT_ROWS) * cw_ref[k:k + 1, cols]
                    cv_ref[pl.ds(r0 + half * SHIFT_ROWS, SHIFT_ROWS), cols] = acc + cb_ref[:, cols]
        _pipelined_row_tiles(h_ref, w_ref, z_refs, glu_conv)

    @pl.when(s == 2)
    def _():
        def norm_gate(r0, z):
            rows = pl.ds(r0, TR)
            y = _silu(_layernorm(cv_ref[rows, :], lng_ref[...], lnb_ref[...]))
            o_ref[rows, :] = (y * _silu(z)).astype(BF16)
        _row_tiles(h_ref, w_ref, norm_gate)


def _branch_b(h3, w_in_bf, l, conv_w, conv_b, ln_g, ln_b):
    b_sz, t_len, _ = h3.shape
    return pl.pallas_call(
        _branch_b_kernel,
        grid=(b_sz, 3),
        in_specs=[pl.BlockSpec((None, t_len, D), lambda b, s: (b, 0, 0)),
                  pl.BlockSpec((None, D, W), lambda b, s: (l, 0, 3 + s)),
                  pl.BlockSpec((CONV_W, W), lambda b, s: (0, 0)),
                  pl.BlockSpec((1, W), lambda b, s: (0, 0)),
                  pl.BlockSpec((1, W), lambda b, s: (0, 0)),
                  pl.BlockSpec((1, W), lambda b, s: (0, 0))],
        out_specs=pl.BlockSpec((None, t_len, W), lambda b, s: (b, 0, 0)),
        out_shape=jax.ShapeDtypeStruct((b_sz, t_len, W), BF16),
        scratch_shapes=[pltpu.VMEM((N_SLAB, PAD + t_len, LANE), F32),
                        pltpu.VMEM((t_len, W), F32)] + Z_SCRATCH,
        compiler_params=_params(("arbitrary", "arbitrary"), 60),
        name="branch_b",
    )(h3, w_in_bf, conv_w, conv_b.reshape(1, W), ln_g.reshape(1, W), ln_b.reshape(1, W))


def _branch_c_kernel(h_ref, w_ref, pw_ref, scale_ref, o_ref, sc_ref, yc_ref, z0_ref, z1_ref):
    s = pl.program_id(1)
    z_refs = (z0_ref, z1_ref)

    @pl.when(s == 0)
    def _():
        sc_ref[:, pl.ds(0, PAD), :] = jnp.zeros((N_SLAB, PAD, LANE), F32)

        def pool(r0, z_ref):
            _slab_store(sc_ref, PAD + r0, z_ref[...])
            for half in range(TR // SHIFT_ROWS):
                t0 = r0 + half * SHIFT_ROWS
                t1 = (t0 + 1 + lax.broadcasted_iota(jnp.int32, (SHIFT_ROWS, POOL_GW), 0)).astype(F32)
                for g, win in enumerate(POOL_WINDOWS):
                    cols = pl.ds(g * POOL_GW, POOL_GW)
                    slabs = range(g * POOL_GW // LANE, (g + 1) * POOL_GW // LANE)

                    def window(shift):
                        return jnp.concatenate(
                            [_shifted_rows(sc_ref, j, PAD + t0 - shift, SHIFT_ROWS) for j in slabs], axis=1)
                    cur = window(0)
                    wsum = cur
                    for k in range(1, win):
                        wsum = wsum + window(k)
                    diff = wsum / jnp.minimum(t1, float(win)) - cur
                    y = jnp.dot(diff.astype(BF16), pw_ref[g].astype(BF16), preferred_element_type=F32)
                    yc_ref[pl.ds(t0, SHIFT_ROWS), cols] = y * scale_ref[:, cols]
        _pipelined_row_tiles(h_ref, w_ref, z_refs, pool)

    @pl.when(s == 1)
    def _():
        def silu_gate(r0, z):
            rows = pl.ds(r0, TR)
            o_ref[rows, :] = (yc_ref[rows, :] * _silu(z)).astype(BF16)
        _row_tiles(h_ref, w_ref, silu_gate)


def _branch_c(h3, w_in_bf, l, pool_w, pool_scale):
    b_sz, t_len, _ = h3.shape
    n_groups = len(POOL_WINDOWS)
    return pl.pallas_call(
        _branch_c_kernel,
        grid=(b_sz, 2),
        in_specs=[pl.BlockSpec((None, t_len, D), lambda b, s: (b, 0, 0)),
                  pl.BlockSpec((None, D, W), lambda b, s: (l, 0, 6 + s)),
                  pl.BlockSpec((n_groups, POOL_GW, POOL_GW), lambda b, s: (0, 0, 0)),
                  pl.BlockSpec((1, W), lambda b, s: (0, 0))],
        out_specs=pl.BlockSpec((None, t_len, W), lambda b, s: (b, 0, 0)),
        out_shape=jax.ShapeDtypeStruct((b_sz, t_len, W), BF16),
        scratch_shapes=[pltpu.VMEM((N_SLAB, PAD + t_len, LANE), F32),
                        pltpu.VMEM((t_len, W), F32)] + Z_SCRATCH,
        compiler_params=_params(("arbitrary", "arbitrary"), 56),
        name="branch_c",
    )(h3, w_in_bf, pool_w, pool_scale.reshape(1, W))


def _branch_m_kernel(h_ref, w_ref, kv_ref, o_ref, sc_ref, z0_ref, z1_ref):
    s = pl.program_id(1)
    z_refs = (z0_ref, z1_ref)

    @pl.when(s == 0)
    def _():
        def attend(r0, z_ref):
            rows = pl.ds(r0, TR)
            q = (z_ref[...] * (HEAD_DIM ** -0.5)).astype(BF16)
            for hd in range(HEADS):
                cols = pl.ds(hd * HEAD_DIM, HEAD_DIM)
                sc = lax.dot_general(q[:, hd * HEAD_DIM:(hd + 1) * HEAD_DIM], kv_ref[:, cols],
                                     (((1,), (1,)), ((), ())), preferred_element_type=F32)
                e = jnp.exp(sc - jnp.max(sc, axis=-1, keepdims=True))
                p = (e / jnp.sum(e, axis=-1, keepdims=True)).astype(BF16)
                sc_ref[rows, cols] = jnp.dot(
                    p, kv_ref[:, pl.ds(W + hd * HEAD_DIM, HEAD_DIM)], preferred_element_type=F32)
        _pipelined_row_tiles(h_ref, w_ref, z_refs, attend)

    @pl.when(s == 1)
    def _():
        def silu_gate(r0, z):
            rows = pl.ds(r0, TR)
            o_ref[rows, :] = (sc_ref[rows, :] * _silu(z)).astype(BF16)
        _row_tiles(h_ref, w_ref, silu_gate)


def _branch_m(h3, w_in_bf, l, kv, mem_len):
    b_sz, t_len, _ = h3.shape
    return pl.pallas_call(
        _branch_m_kernel,
        grid=(b_sz, 2),
        in_specs=[pl.BlockSpec((None, t_len, D), lambda b, s: (b, 0, 0)),
                  pl.BlockSpec((None, D, W), lambda b, s: (l, 0, 8 + s)),
                  pl.BlockSpec((None, mem_len, 2 * W), lambda b, s: (l, b, 0))],
        out_specs=pl.BlockSpec((None, t_len, W), lambda b, s: (b, 0, 0)),
        out_shape=jax.ShapeDtypeStruct((b_sz, t_len, W), BF16),
        scratch_shapes=[pltpu.VMEM((t_len, W), F32)] + Z_SCRATCH,
        compiler_params=_params(("arbitrary", "arbitrary"), 56),
        name="branch_m",
    )(h3, w_in_bf, kv)


def _merge_kernel(h_ref, ya_ref, yb_ref, yc_ref, ym_ref,
                  wg0, wg1, wg2, wg3, wb0, wb1, wb2, wb3, o_ref):
    h = h_ref[...]
    acc = None
    for y_ref, wg, wb in ((ya_ref, wg0, wb0), (yb_ref, wg1, wb1),
                          (yc_ref, wg2, wb2), (ym_ref, wg3, wb3)):
        gate = jax.nn.sigmoid(jnp.dot(h, wg[...], preferred_element_type=F32))
        term = gate * jnp.dot(y_ref[...], wb[...], preferred_element_type=F32)
        acc = term if acc is None else acc + term
    o_ref[...] = acc.astype(BF16)


def _merge(h2, ys, w_in_bf, w_branch_bf, l, tm=1024, cn=256):
    n = h2.shape[0]
    tm = min(tm, n)
    gate_blk0 = GATE_COL0 // cn
    blk_per_branch = D // cn

    def wg_spec(nb):
        return pl.BlockSpec((None, D, cn), lambda i, c: (l, 0, gate_blk0 + nb * blk_per_branch + c))

    def wb_spec(nb):
        return pl.BlockSpec((None, None, W, cn), lambda i, c: (l, nb, 0, c))

    y_spec = pl.BlockSpec((tm, W), lambda i, c: (i, 0))
    return pl.pallas_call(
        _merge_kernel,
        grid=(n // tm, D // cn),
        in_specs=[pl.BlockSpec((tm, D), lambda i, c: (i, 0))] + [y_spec] * N_BRANCH
                 + [wg_spec(nb) for nb in range(N_BRANCH)]
                 + [wb_spec(nb) for nb in range(N_BRANCH)],
        out_specs=pl.BlockSpec((tm, cn), lambda i, c: (i, c)),
        out_shape=jax.ShapeDtypeStruct((n, D), BF16),
        compiler_params=_params(("arbitrary", "arbitrary"), 56),
        name="gate_merge",
    )(h2, *ys, *([w_in_bf] * N_BRANCH), *([w_branch_bf] * N_BRANCH))


def _outproj_kernel(x_ref, m_ref, w_ref, g_ref, *out_refs, sub, last):
    def body(r, carry):
        rows = pl.ds(pl.multiple_of(r * sub, sub), sub)
        xn = x_ref[rows, :] + jnp.dot(m_ref[rows, :], w_ref[...], preferred_element_type=F32)
        normed = _rms(xn, g_ref[...])
        if last:
            out_refs[0][rows, :] = normed
        else:
            out_refs[0][rows, :] = xn
            out_refs[1][rows, :] = normed.astype(BF16)
        return carry
    lax.fori_loop(0, x_ref.shape[0] // sub, body, 0)


def _outproj(x2, merged, w_out_bf, l, g_next, last, tm=512, sub=256):
    n = x2.shape[0]
    tm = min(tm, n)
    row_spec = pl.BlockSpec((tm, D), lambda i: (i, 0))
    if last:
        out_specs, out_shape = row_spec, jax.ShapeDtypeStruct((n, D), F32)
    else:
        out_specs = [row_spec, row_spec]
        out_shape = [jax.ShapeDtypeStruct((n, D), F32), jax.ShapeDtypeStruct((n, D), BF16)]
    return pl.pallas_call(
        functools.partial(_outproj_kernel, sub=min(sub, tm), last=last),
        grid=(n // tm,),
        in_specs=[row_spec, row_spec,
                  pl.BlockSpec((None, D, D), lambda i: (l, 0, 0)),
                  pl.BlockSpec((1, D), lambda i: (0, 0))],
        out_specs=out_specs,
        out_shape=out_shape,
        compiler_params=_params(("arbitrary",), 56),
        name="out_proj",
    )(x2, merged, w_out_bf, g_next.reshape(1, D))


def kernel(x, mem, norm_g, w_in, gmlp_ln_g, gmlp_ln_b, gmlp_ws, gmlp_bs, conv_w, conv_b,
           conv_ln_g, conv_ln_b, pool_w, pool_scale, mem_norm_g, w_kv, w_branch, w_out, final_g):
    b_sz, t_len, d = x.shape
    n_layers = w_in.shape[0]
    mem_len = mem.shape[1]
    assert d == D and w_in.shape[2] == GATE_COL0 + N_BRANCH * D
    assert t_len % (2 * TR) == 0 and TR % CHUNK == 0

    w_in_bf = w_in.astype(BF16)
    w_kv_bf = w_kv.astype(BF16)
    w_branch_bf = w_branch.astype(BF16)
    w_out_bf = w_out.astype(BF16)

    n = b_sz * t_len
    x2 = x.reshape(n, D)
    kv = _memory_kv(mem.reshape(b_sz * mem_len, D), mem_norm_g, w_kv_bf)
    h2 = _rmsnorm(x2, norm_g[0], BF16)

    for l in range(n_layers):
        h3 = h2.reshape(b_sz, t_len, D)
        bias_full = jnp.repeat(gmlp_bs[l].T, LANE, axis=1)
        ys = (
            _branch_a(h3, w_in_bf, l, gmlp_ln_g[l], gmlp_ln_b[l], gmlp_ws[l], bias_full),
            _branch_b(h3, w_in_bf, l, conv_w[l], conv_b[l], conv_ln_g[l], conv_ln_b[l]),
            _branch_c(h3, w_in_bf, l, pool_w[l], pool_scale[l]),
            _branch_m(h3, w_in_bf, l, kv, mem_len),
        )
        ys = tuple(y.reshape(n, W) for y in ys)
        merged = _merge(h2, ys, w_in_bf, w_branch_bf, l)
        last = l == n_layers - 1
        g_next = final_g if last else norm_g[l + 1]
        res = _outproj(x2, merged, w_out_bf, l, g_next, last)
        if last:
            return res.reshape(b_sz, t_len, D)
        x2, h2 = res
```

```python
import functools

import jax
import jax.numpy as jnp
from jax import lax
from jax.experimental import pallas as pl
from jax.experimental.pallas import tpu as pltpu

F32 = jnp.float32
BF16 = jnp.bfloat16

D = 2048
W = 1024
CHUNK = 128
G_GMLP = 8
CONV_W = 31
POOL_WINDOWS = (2, 4, 8, 16)
POOL_GW = 256
HEADS = 4
HEAD_DIM = 256
EPS = 1e-6
N_BRANCH = 4
LANE = 128
BF16_TILE_ROWS = 16

PAD = 32
SHIFT_ROWS = 128
CONV_SPLIT = 19
TR = 256
GATE_COL0 = 10 * W

MIB = 1024 * 1024


def _params(sem, vmem_mib):
    return pltpu.CompilerParams(dimension_semantics=sem, vmem_limit_bytes=vmem_mib * MIB)


def _layernorm(x, g, b):
    mu = jnp.mean(x, axis=-1, keepdims=True)
    xc = x - mu
    var = jnp.mean(xc * xc, axis=-1, keepdims=True)
    return xc * lax.rsqrt(var + EPS) * g + b


def _rms(x, g):
    return x * lax.rsqrt(jnp.mean(x * x, axis=-1, keepdims=True) + EPS) * g


def _silu(x):
    return x * jax.nn.sigmoid(x)


def _row_tiles(h_ref, w_ref, epilogue):
    def step(r, carry):
        r0 = pl.multiple_of(r * TR, TR)
        epilogue(r0, jnp.dot(h_ref[pl.ds(r0, TR), :], w_ref[...], preferred_element_type=F32))
        return carry
    lax.fori_loop(0, h_ref.shape[0] // TR, step, 0, unroll=2)


def _pipelined_row_tiles(h_ref, w_ref, z_refs, epilogue):
    t_len = h_ref.shape[0]
    n_tiles = t_len // TR
    z0_ref, z1_ref = z_refs

    def project(r0, z_ref):
        z_ref[...] = jnp.dot(h_ref[pl.ds(r0, TR), :], w_ref[...], preferred_element_type=F32)

    project(0, z0_ref)

    def step(i, carry):
        r0 = pl.multiple_of(i * (2 * TR), 2 * TR)
        project(r0 + TR, z1_ref)
        epilogue(r0, z0_ref)
        project(r0 + 2 * TR, z0_ref)
        epilogue(r0 + TR, z1_ref)
        return carry
    lax.fori_loop(0, n_tiles // 2 - 1, step, 0)
    r0 = t_len - 2 * TR
    project(r0 + TR, z1_ref)
    epilogue(r0, z0_ref)
    epilogue(r0 + TR, z1_ref)


Z_SCRATCH = [pltpu.VMEM((TR, W), F32), pltpu.VMEM((TR, W), F32)]
N_SLAB = W // LANE


def _slab_store(ref, start, value):
    for j in range(N_SLAB):
        ref[j, pl.ds(start, value.shape[0]), :] = value[:, j * LANE:(j + 1) * LANE]


def _shifted_rows(ref, slab, start, n_rows):
    return ref[slab, pl.ds(start, n_rows, stride=1), :]


def _rmsnorm_kernel(x_ref, g_ref, o_ref):
    o_ref[...] = _rms(x_ref[...], g_ref[...]).astype(o_ref.dtype)


def _rmsnorm(x2d, g, out_dtype, tm=512):
    n = x2d.shape[0]
    return pl.pallas_call(
        _rmsnorm_kernel,
        grid=(n // tm,),
        in_specs=[pl.BlockSpec((tm, D), lambda i: (i, 0)),
                  pl.BlockSpec((1, D), lambda i: (0, 0))],
        out_specs=pl.BlockSpec((tm, D), lambda i: (i, 0)),
        out_shape=jax.ShapeDtypeStruct((n, D), out_dtype),
        compiler_params=_params(("arbitrary",), 40),
        name="rmsnorm",
    )(x2d, g.reshape(1, D))


def _kv_kernel(m_ref, g_ref, w_ref, o_ref):
    mn = _rms(m_ref[...], g_ref[...]).astype(BF16)
    o_ref[...] = jnp.dot(mn, w_ref[...], preferred_element_type=F32).astype(BF16)


def _memory_kv(mem2d, mem_norm_g, w_kv_bf, tm=512):
    n_layers = w_kv_bf.shape[0]
    n = mem2d.shape[0]
    tm = min(tm, n)
    return pl.pallas_call(
        _kv_kernel,
        grid=(n_layers, n // tm),
        in_specs=[pl.BlockSpec((tm, D), lambda l, i: (i, 0)),
                  pl.BlockSpec((None, 1, D), lambda l, i: (l, 0, 0)),
                  pl.BlockSpec((None, D, 2 * W), lambda l, i: (l, 0, 0))],
        out_specs=pl.BlockSpec((None, tm, 2 * W), lambda l, i: (l, i, 0)),
        out_shape=jax.ShapeDtypeStruct((n_layers, n, 2 * W), BF16),
        compiler_params=_params(("arbitrary", "arbitrary"), 48),
        name="memory_kv",
    )(mem2d, mem_norm_g.reshape(n_layers, 1, D), w_kv_bf)


def _branch_a_kernel(h_ref, w_ref, lng_ref, lnb_ref, ws_ref, bias_ref, o_ref,
                     sc_ref, wsm_ref, vln_ref, z0_ref, z1_ref):
    s = pl.program_id(1)
    z_refs = (z0_ref, z1_ref)

    @pl.when(s == 0)
    def _():
        row = lax.broadcasted_iota(jnp.int32, (CHUNK, CHUNK), 0)
        col = lax.broadcasted_iota(jnp.int32, (CHUNK, CHUNK), 1)
        for g in range(G_GMLP):
            wsm_ref[g] = jnp.where(row >= col, ws_ref[g], 0.0).astype(BF16)

        def gelu_u(r0, z):
            sc_ref[pl.ds(r0, TR), :] = jax.nn.gelu(z)
        _row_tiles(h_ref, w_ref, gelu_u)

    @pl.when(s == 1)
    def _():
        def spatial_gate(r0, z_ref):
            v = _layernorm(jax.nn.gelu(z_ref[...]), lng_ref[...], lnb_ref[...])
            vln_ref[...] = v.astype(BF16)
            for c in range(TR // CHUNK):
                for g in range(G_GMLP):
                    rows = pl.ds(r0 + c * CHUNK, CHUNK)
                    cols = pl.ds(g * LANE, LANE)
                    sp = jnp.dot(wsm_ref[g], vln_ref[pl.ds(c * CHUNK, CHUNK), cols],
                                 preferred_element_type=F32) + bias_ref[:, cols]
                    sc_ref[rows, cols] = sc_ref[rows, cols] * sp
        _pipelined_row_tiles(h_ref, w_ref, z_refs, spatial_gate)

    @pl.when(s == 2)
    def _():
        def silu_gate(r0, z):
            rows = pl.ds(r0, TR)
            o_ref[rows, :] = (sc_ref[rows, :] * _silu(z)).astype(BF16)
        _row_tiles(h_ref, w_ref, silu_gate)


def _branch_a(h3, w_in_bf, ln_g, ln_b, ws, bias_full):
    b_sz, t_len, _ = h3.shape
    return pl.pallas_call(
        _branch_a_kernel,
        grid=(b_sz, 3),
        in_specs=[pl.BlockSpec((None, t_len, D), lambda b, s: (b, 0, 0)),
                  pl.BlockSpec((D, W), lambda b, s: (0, s)),
                  pl.BlockSpec((1, W), lambda b, s: (0, 0)),
                  pl.BlockSpec((1, W), lambda b, s: (0, 0)),
                  pl.BlockSpec((G_GMLP, CHUNK, CHUNK), lambda b, s: (0, 0, 0)),
                  pl.BlockSpec((CHUNK, W), lambda b, s: (0, 0))],
        out_specs=pl.BlockSpec((None, t_len, W), lambda b, s: (b, 0, 0)),
        out_shape=jax.ShapeDtypeStruct((b_sz, t_len, W), BF16),
        scratch_shapes=[pltpu.VMEM((t_len, W), F32),
                        pltpu.VMEM((G_GMLP, CHUNK, CHUNK), BF16),
                        pltpu.VMEM((TR, W), BF16)] + Z_SCRATCH,
        compiler_params=_params(("arbitrary", "arbitrary"), 56),
        name="branch_a",
    )(h3, w_in_bf, ln_g.reshape(1, W), ln_b.reshape(1, W), ws, bias_full)


def _branch_b_kernel(h_ref, w_ref, cw_ref, cb_ref, lng_ref, lnb_ref, o_ref,
                     sc_ref, cv_ref, z0_ref, z1_ref):
    s = pl.program_id(1)
    z_refs = (z0_ref, z1_ref)

    def conv_taps(r0, k_lo, k_hi, init_ref):
        for j in range(N_SLAB):
            cols = pl.ds(j * LANE, LANE)
            for half in range(TR // SHIFT_ROWS):
                rows = pl.ds(r0 + half * SHIFT_ROWS, SHIFT_ROWS)
                base = r0 + half * SHIFT_ROWS + PAD - (CONV_W - 1)
                acc = init_ref[rows, cols] if init_ref is cv_ref else init_ref[:, cols]
                for k in range(k_lo, k_hi):
                    acc = acc + _shifted_rows(sc_ref, j, base + k, SHIFT_ROWS) * cw_ref[k:k + 1, cols]
                cv_ref[rows, cols] = acc

    @pl.when(s == 0)
    def _():
        sc_ref[:, pl.ds(0, PAD), :] = jnp.zeros((N_SLAB, PAD, LANE), F32)

        def keep(r0, z):
            _slab_store(sc_ref, PAD + r0, z)
        _row_tiles(h_ref, w_ref, keep)

    @pl.when(s == 1)
    def _():
        def glu_conv(r0, z_ref):
            gate = jax.nn.sigmoid(z_ref[...])
            for j in range(N_SLAB):
                rows = pl.ds(PAD + r0, TR)
                sc_ref[j, rows, :] = sc_ref[j, rows, :] * gate[:, j * LANE:(j + 1) * LANE]
            conv_taps(r0, 0, CONV_SPLIT, cb_ref)
        _pipelined_row_tiles(h_ref, w_ref, z_refs, glu_conv)

    @pl.when(s == 2)
    def _():
        def norm_gate(r0, z):
            rows = pl.ds(r0, TR)
            conv_taps(r0, CONV_SPLIT, CONV_W, cv_ref)
            y = _silu(_layernorm(cv_ref[rows, :], lng_ref[...], lnb_ref[...]))
            o_ref[rows, :] = (y * _silu(z)).astype(BF16)
        _row_tiles(h_ref, w_ref, norm_gate)


def _branch_b(h3, w_in_bf, conv_w, conv_b, ln_g, ln_b):
    b_sz, t_len, _ = h3.shape
    return pl.pallas_call(
        _branch_b_kernel,
        grid=(b_sz, 3),
        in_specs=[pl.BlockSpec((None, t_len, D), lambda b, s: (b, 0, 0)),
                  pl.BlockSpec((D, W), lambda b, s: (0, 3 + s)),
                  pl.BlockSpec((CONV_W, W), lambda b, s: (0, 0)),
                  pl.BlockSpec((1, W), lambda b, s: (0, 0)),
                  pl.BlockSpec((1, W), lambda b, s: (0, 0)),
                  pl.BlockSpec((1, W), lambda b, s: (0, 0))],
        out_specs=pl.BlockSpec((None, t_len, W), lambda b, s: (b, 0, 0)),
        out_shape=jax.ShapeDtypeStruct((b_sz, t_len, W), BF16),
        scratch_shapes=[pltpu.VMEM((N_SLAB, PAD + t_len, LANE), F32),
                        pltpu.VMEM((t_len, W), F32)] + Z_SCRATCH,
        compiler_params=_params(("arbitrary", "arbitrary"), 60),
        name="branch_b",
    )(h3, w_in_bf, conv_w, conv_b.reshape(1, W), ln_g.reshape(1, W), ln_b.reshape(1, W))


def _branch_c_kernel(h_ref, w_ref, pw_ref, scale_ref, o_ref, sc_ref, yc_ref, z0_ref, z1_ref):
    s = pl.program_id(1)
    z_refs = (z0_ref, z1_ref)

    @pl.when(s == 0)
    def _():
        sc_ref[:, pl.ds(0, PAD), :] = jnp.zeros((N_SLAB, PAD, LANE), F32)

        def pool(r0, z_ref):
            _slab_store(sc_ref, PAD + r0, z_ref[...])
            for half in range(TR // SHIFT_ROWS):
                t0 = r0 + half * SHIFT_ROWS
                t1 = (t0 + 1 + lax.broadcasted_iota(jnp.int32, (SHIFT_ROWS, POOL_GW), 0)).astype(F32)
                for g, win in enumerate(POOL_WINDOWS):
                    cols = pl.ds(g * POOL_GW, POOL_GW)
                    slabs = range(g * POOL_GW // LANE, (g + 1) * POOL_GW // LANE)

                    def window(shift):
                        return jnp.concatenate(
                            [_shifted_rows(sc_ref, j, PAD + t0 - shift, SHIFT_ROWS) for j in slabs], axis=1)
                    cur = window(0)
                    wsum = cur
                    for k in range(1, win):
                        wsum = wsum + window(k)
                    diff = wsum / jnp.minimum(t1, float(win)) - cur
                    y = jnp.dot(diff.astype(BF16), pw_ref[g].astype(BF16), preferred_element_type=F32)
                    yc_ref[pl.ds(t0, SHIFT_ROWS), cols] = y * scale_ref[:, cols]
        _pipelined_row_tiles(h_ref, w_ref, z_refs, pool)

    @pl.when(s == 1)
    def _():
        def silu_gate(r0, z):
            rows = pl.ds(r0, TR)
            o_ref[rows, :] = (yc_ref[rows, :] * _silu(z)).astype(BF16)
        _row_tiles(h_ref, w_ref, silu_gate)


def _branch_c(h3, w_in_bf, pool_w, pool_scale):
    b_sz, t_len, _ = h3.shape
    n_groups = len(POOL_WINDOWS)
    return pl.pallas_call(
        _branch_c_kernel,
        grid=(b_sz, 2),
        in_specs=[pl.BlockSpec((None, t_len, D), lambda b, s: (b, 0, 0)),
                  pl.BlockSpec((D, W), lambda b, s: (0, 6 + s)),
                  pl.BlockSpec((n_groups, POOL_GW, POOL_GW), lambda b, s: (0, 0, 0)),
                  pl.BlockSpec((1, W), lambda b, s: (0, 0))],
        out_specs=pl.BlockSpec((None, t_len, W), lambda b, s: (b, 0, 0)),
        out_shape=jax.ShapeDtypeStruct((b_sz, t_len, W), BF16),
        scratch_shapes=[pltpu.VMEM((N_SLAB, PAD + t_len, LANE), F32),
                        pltpu.VMEM((t_len, W), F32)] + Z_SCRATCH,
        compiler_params=_params(("arbitrary", "arbitrary"), 56),
        name="branch_c",
    )(h3, w_in_bf, pool_w, pool_scale.reshape(1, W))


def _branch_m_kernel(h_ref, w_ref, kv_ref, o_ref, sc_ref, z0_ref, z1_ref):
    s = pl.program_id(1)
    z_refs = (z0_ref, z1_ref)

    @pl.when(s == 0)
    def _():
        def attend(r0, z_ref):
            rows = pl.ds(r0, TR)
            q = (z_ref[...] * (HEAD_DIM ** -0.5)).astype(BF16)
            for hd in range(HEADS):
                cols = pl.ds(hd * HEAD_DIM, HEAD_DIM)
                sc = lax.dot_general(q[:, hd * HEAD_DIM:(hd + 1) * HEAD_DIM], kv_ref[:, cols],
                                     (((1,), (1,)), ((), ())), preferred_element_type=F32)
                e = jnp.exp(sc - jnp.max(sc, axis=-1, keepdims=True))
                p = (e / jnp.sum(e, axis=-1, keepdims=True)).astype(BF16)
                sc_ref[rows, cols] = jnp.dot(
                    p, kv_ref[:, pl.ds(W + hd * HEAD_DIM, HEAD_DIM)], preferred_element_type=F32)
        _pipelined_row_tiles(h_ref, w_ref, z_refs, attend)

    @pl.when(s == 1)
    def _():
        def silu_gate(r0, z):
            rows = pl.ds(r0, TR)
            o_ref[rows, :] = (sc_ref[rows, :] * _silu(z)).astype(BF16)
        _row_tiles(h_ref, w_ref, silu_gate)


def _branch_m(h3, w_in_bf, kv, l, mem_len):
    b_sz, t_len, _ = h3.shape
    return pl.pallas_call(
        _branch_m_kernel,
        grid=(b_sz, 2),
        in_specs=[pl.BlockSpec((None, t_len, D), lambda b, s: (b, 0, 0)),
                  pl.BlockSpec((D, W), lambda b, s: (0, 8 + s)),
                  pl.BlockSpec((None, mem_len, 2 * W), lambda b, s: (l, b, 0))],
        out_specs=pl.BlockSpec((None, t_len, W), lambda b, s: (b, 0, 0)),
        out_shape=jax.ShapeDtypeStruct((b_sz, t_len, W), BF16),
        scratch_shapes=[pltpu.VMEM((t_len, W), F32)] + Z_SCRATCH,
        compiler_params=_params(("arbitrary", "arbitrary"), 56),
        name="branch_m",
    )(h3, w_in_bf, kv)


def _merge_kernel(*refs, n_cast):
    h_ref, y_refs, wg_refs, wb_refs = refs[0], refs[1:5], refs[5:9], refs[9:13]
    cast_in, o_ref, cast_out = refs[13:13 + n_cast], refs[13 + n_cast], refs[14 + n_cast:]
    h = h_ref[...]
    acc = None
    for y_ref, wg, wb in zip(y_refs, wg_refs, wb_refs):
        gate = jax.nn.sigmoid(jnp.dot(h, wg[...], preferred_element_type=F32))
        term = gate * jnp.dot(y_ref[...], wb[...], preferred_element_type=F32)
        acc = term if acc is None else acc + term
    o_ref[...] = acc.astype(BF16)
    for src_ref, dst_ref in zip(cast_in, cast_out):
        dst_ref[...] = src_ref[...].astype(BF16)


def _merge(h2, ys, w_in_bf, w_branch_bf, next_weights_f32, l_next, tm=1024, cn=256):
    n = h2.shape[0]
    tm = min(tm, n)
    n_col = D // cn
    n_steps = (n // tm) * n_col
    gate_blk0 = GATE_COL0 // cn

    def wg_spec(nb):
        return pl.BlockSpec((D, cn), lambda i, c: (0, gate_blk0 + nb * n_col + c))

    def wb_spec(nb):
        return pl.BlockSpec((W, cn), lambda i, c: (nb, c))

    cast_in_specs, cast_out_specs, cast_out_shapes = [], [], []
    for w in next_weights_f32:
        rows, cols = w.shape[1:]
        assert rows % (n_steps * BF16_TILE_ROWS) == 0
        cast_in_specs.append(pl.BlockSpec((None, rows // n_steps, cols), lambda i, c: (l_next, i * n_col + c, 0)))
        cast_out_specs.append(pl.BlockSpec((rows // n_steps, cols), lambda i, c: (i * n_col + c, 0)))
        cast_out_shapes.append(jax.ShapeDtypeStruct((rows, cols), BF16))

    y_spec = pl.BlockSpec((tm, W), lambda i, c: (i, 0))
    outs = pl.pallas_call(
        functools.partial(_merge_kernel, n_cast=len(next_weights_f32)),
        grid=(n // tm, n_col),
        in_specs=[pl.BlockSpec((tm, D), lambda i, c: (i, 0))] + [y_spec] * N_BRANCH
                 + [wg_spec(nb) for nb in range(N_BRANCH)]
                 + [wb_spec(nb) for nb in range(N_BRANCH)] + cast_in_specs,
        out_specs=[pl.BlockSpec((tm, cn), lambda i, c: (i, c))] + cast_out_specs,
        out_shape=[jax.ShapeDtypeStruct((n, D), BF16)] + cast_out_shapes,
        compiler_params=_params(("arbitrary", "arbitrary"), 56),
        name="gate_merge",
    )(h2, *ys, *([w_in_bf] * N_BRANCH), *([w_branch_bf] * N_BRANCH), *next_weights_f32)
    return outs[0], tuple(outs[1:])


def _outproj_kernel(x_ref, m_ref, w_ref, g_ref, *out_refs, sub, last):
    def body(r, carry):
        rows = pl.ds(pl.multiple_of(r * sub, sub), sub)
        xn = x_ref[rows, :] + jnp.dot(m_ref[rows, :], w_ref[...], preferred_element_type=F32)
        normed = _rms(xn, g_ref[...])
        if last:
            out_refs[0][rows, :] = normed
        else:
            out_refs[0][rows, :] = xn
            out_refs[1][rows, :] = normed.astype(BF16)
        return carry
    lax.fori_loop(0, x_ref.shape[0] // sub, body, 0, unroll=2)


def _outproj(x2, merged, w_out_bf, g_next, last, tm=512, sub=256):
    n = x2.shape[0]
    tm = min(tm, n)
    row_spec = pl.BlockSpec((tm, D), lambda i: (i, 0))
    if last:
        out_specs, out_shape = row_spec, jax.ShapeDtypeStruct((n, D), F32)
    else:
        out_specs = [row_spec, row_spec]
        out_shape = [jax.ShapeDtypeStruct((n, D), F32), jax.ShapeDtypeStruct((n, D), BF16)]
    return pl.pallas_call(
        functools.partial(_outproj_kernel, sub=min(sub, tm), last=last),
        grid=(n // tm,),
        in_specs=[row_spec, row_spec,
                  pl.BlockSpec((D, D), lambda i: (0, 0)),
                  pl.BlockSpec((1, D), lambda i: (0, 0))],
        out_specs=out_specs,
        out_shape=out_shape,
        compiler_params=_params(("arbitrary",), 56),
        name="out_proj",
    )(x2, merged, w_out_bf, g_next.reshape(1, D))


def kernel(x, mem, norm_g, w_in, gmlp_ln_g, gmlp_ln_b, gmlp_ws, gmlp_bs, conv_w, conv_b,
           conv_ln_g, conv_ln_b, pool_w, pool_scale, mem_norm_g, w_kv, w_branch, w_out, final_g):
    b_sz, t_len, d = x.shape
    n_layers = w_in.shape[0]
    mem_len = mem.shape[1]
    assert d == D and w_in.shape[2] == GATE_COL0 + N_BRANCH * D
    assert t_len % (2 * TR) == 0 and TR % CHUNK == 0

    w_branch2 = w_branch.reshape(n_layers, N_BRANCH * W, D)
    layer_w = (w_in[0].astype(BF16), w_branch2[0].astype(BF16), w_out[0].astype(BF16))
    w_kv_bf = w_kv.astype(BF16)

    n = b_sz * t_len
    x2 = x.reshape(n, D)
    kv = _memory_kv(mem.reshape(b_sz * mem_len, D), mem_norm_g, w_kv_bf)
    h2 = _rmsnorm(x2, norm_g[0], BF16)

    for l in range(n_layers):
        w_in_bf, w_branch_bf, w_out_bf = layer_w
        last = l == n_layers - 1
        h3 = h2.reshape(b_sz, t_len, D)
        bias_full = jnp.repeat(gmlp_bs[l].T, LANE, axis=1)
        ys = (
            _branch_a(h3, w_in_bf, gmlp_ln_g[l], gmlp_ln_b[l], gmlp_ws[l], bias_full),
            _branch_b(h3, w_in_bf, conv_w[l], conv_b[l], conv_ln_g[l], conv_ln_b[l]),
            _branch_c(h3, w_in_bf, pool_w[l], pool_scale[l]),
            _branch_m(h3, w_in_bf, kv, l, mem_len),
        )
        ys = tuple(y.reshape(n, W) for y in ys)
        next_f32 = () if last else (w_in, w_branch2, w_out)
        merged, layer_w = _merge(h2, ys, w_in_bf, w_branch_bf, next_f32, l + 1)
        g_next = final_g if last else norm_g[l + 1]
        res = _outproj(x2, merged, w_out_bf, g_next, last)
        if last:
            return res.reshape(b_sz, t_len, D)
        x2, h2 = res
```

```python
import functools

import jax
import jax.numpy as jnp
from jax import lax
from jax.experimental import pallas as pl
from jax.experimental.pallas import tpu as pltpu

F32 = jnp.float32
BF16 = jnp.bfloat16

D = 2048
W = 1024
CHUNK = 128
G_GMLP = 8
CONV_W = 31
POOL_WINDOWS = (2, 4, 8, 16)
POOL_GW = 256
HEADS = 4
HEAD_DIM = 256
EPS = 1e-6
N_BRANCH = 4
LANE = 128
BF16_TILE_ROWS = 16

PAD = 32
SHIFT_ROWS = 128
CONV_SPLIT = 19
TR = 256
GATE_COL0 = 10 * W

MIB = 1024 * 1024


def _params(sem, vmem_mib):
    return pltpu.CompilerParams(dimension_semantics=sem, vmem_limit_bytes=vmem_mib * MIB)


def _layernorm(x, g, b):
    mu = jnp.mean(x, axis=-1, keepdims=True)
    xc = x - mu
    var = jnp.mean(xc * xc, axis=-1, keepdims=True)
    return xc * lax.rsqrt(var + EPS) * g + b


def _rms(x, g):
    return x * lax.rsqrt(jnp.mean(x * x, axis=-1, keepdims=True) + EPS) * g


def _silu(x):
    return x * jax.nn.sigmoid(x)


def _row_tiles(h_ref, w_ref, epilogue):
    for r in range(h_ref.shape[0] // TR):
        r0 = r * TR
        epilogue(r0, jnp.dot(h_ref[pl.ds(r0, TR), :], w_ref[...], preferred_element_type=F32))


def _pipelined_row_tiles(h_ref, w_ref, z_refs, epilogue):
    t_len = h_ref.shape[0]
    n_tiles = t_len // TR
    z0_ref, z1_ref = z_refs

    def project(r0, z_ref):
        z_ref[...] = jnp.dot(h_ref[pl.ds(r0, TR), :], w_ref[...], preferred_element_type=F32)

    project(0, z0_ref)

    def step(i, carry):
        r0 = pl.multiple_of(i * (2 * TR), 2 * TR)
        project(r0 + TR, z1_ref)
        epilogue(r0, z0_ref)
        project(r0 + 2 * TR, z0_ref)
        epilogue(r0 + TR, z1_ref)
        return carry
    lax.fori_loop(0, n_tiles // 2 - 1, step, 0)
    r0 = t_len - 2 * TR
    project(r0 + TR, z1_ref)
    epilogue(r0, z0_ref)
    epilogue(r0 + TR, z1_ref)


Z_SCRATCH = [pltpu.VMEM((TR, W), F32), pltpu.VMEM((TR, W), F32)]
N_SLAB = W // LANE


def _slab_store(ref, start, value):
    for j in range(N_SLAB):
        ref[j, pl.ds(start, value.shape[0]), :] = value[:, j * LANE:(j + 1) * LANE]


def _shifted_rows(ref, slab, start, n_rows):
    return ref[slab, pl.ds(start, n_rows, stride=1), :]


def _rmsnorm_kernel(x_ref, g_ref, o_ref):
    o_ref[...] = _rms(x_ref[...], g_ref[...]).astype(o_ref.dtype)


def _rmsnorm(x2d, g, out_dtype, tm=512):
    n = x2d.shape[0]
    return pl.pallas_call(
        _rmsnorm_kernel,
        grid=(n // tm,),
        in_specs=[pl.BlockSpec((tm, D), lambda i: (i, 0)),
                  pl.BlockSpec((1, D), lambda i: (0, 0))],
        out_specs=pl.BlockSpec((tm, D), lambda i: (i, 0)),
        out_shape=jax.ShapeDtypeStruct((n, D), out_dtype),
        compiler_params=_params(("arbitrary",), 40),
        name="rmsnorm",
    )(x2d, g.reshape(1, D))


def _kv_kernel(m_ref, g_ref, w_ref, o_ref):
    mn = _rms(m_ref[...], g_ref[...]).astype(BF16)
    o_ref[...] = jnp.dot(mn, w_ref[...], preferred_element_type=F32).astype(BF16)


def _memory_kv(mem2d, mem_norm_g, w_kv_bf, tm=512):
    n_layers = w_kv_bf.shape[0]
    n = mem2d.shape[0]
    tm = min(tm, n)
    return pl.pallas_call(
        _kv_kernel,
        grid=(n_layers, n // tm),
        in_specs=[pl.BlockSpec((tm, D), lambda l, i: (i, 0)),
                  pl.BlockSpec((None, 1, D), lambda l, i: (l, 0, 0)),
                  pl.BlockSpec((None, D, 2 * W), lambda l, i: (l, 0, 0))],
        out_specs=pl.BlockSpec((None, tm, 2 * W), lambda l, i: (l, i, 0)),
        out_shape=jax.ShapeDtypeStruct((n_layers, n, 2 * W), BF16),
        compiler_params=_params(("arbitrary", "arbitrary"), 48),
        name="memory_kv",
    )(mem2d, mem_norm_g.reshape(n_layers, 1, D), w_kv_bf)


def _branch_a_kernel(h_ref, w_ref, lng_ref, lnb_ref, ws_ref, bias_ref, o_ref,
                     sc_ref, wsm_ref, vln_ref):
    s = pl.program_id(1)

    @pl.when(s == 0)
    def _():
        row = lax.broadcasted_iota(jnp.int32, (CHUNK, CHUNK), 0)
        col = lax.broadcasted_iota(jnp.int32, (CHUNK, CHUNK), 1)
        for g in range(G_GMLP):
            wsm_ref[g] = jnp.where(row >= col, ws_ref[g], 0.0).astype(BF16)

        def gelu_u(r0, z):
            sc_ref[pl.ds(r0, TR), :] = jax.nn.gelu(z)
        _row_tiles(h_ref, w_ref, gelu_u)

    @pl.when(s == 1)
    def _():
        def spatial_gate(r0, z):
            v = _layernorm(jax.nn.gelu(z), lng_ref[...], lnb_ref[...])
            vln_ref[...] = v.astype(BF16)
            for c in range(TR // CHUNK):
                for g in range(G_GMLP):
                    rows = pl.ds(r0 + c * CHUNK, CHUNK)
                    cols = pl.ds(g * LANE, LANE)
                    sp = jnp.dot(wsm_ref[g], vln_ref[pl.ds(c * CHUNK, CHUNK), cols],
                                 preferred_element_type=F32) + bias_ref[:, cols]
                    sc_ref[rows, cols] = sc_ref[rows, cols] * sp
        _row_tiles(h_ref, w_ref, spatial_gate)

    @pl.when(s == 2)
    def _():
        def silu_gate(r0, z):
            rows = pl.ds(r0, TR)
            o_ref[rows, :] = (sc_ref[rows, :] * _silu(z)).astype(BF16)
        _row_tiles(h_ref, w_ref, silu_gate)


def _branch_a(h3, w_in_bf, ln_g, ln_b, ws, bias_full):
    b_sz, t_len, _ = h3.shape
    return pl.pallas_call(
        _branch_a_kernel,
        grid=(b_sz, 3),
        in_specs=[pl.BlockSpec((None, t_len, D), lambda b, s: (b, 0, 0)),
                  pl.BlockSpec((D, W), lambda b, s: (0, s)),
                  pl.BlockSpec((1, W), lambda b, s: (0, 0)),
                  pl.BlockSpec((1, W), lambda b, s: (0, 0)),
                  pl.BlockSpec((G_GMLP, CHUNK, CHUNK), lambda b, s: (0, 0, 0)),
                  pl.BlockSpec((CHUNK, W), lambda b, s: (0, 0))],
        out_specs=pl.BlockSpec((None, t_len, W), lambda b, s: (b, 0, 0)),
        out_shape=jax.ShapeDtypeStruct((b_sz, t_len, W), BF16),
        scratch_shapes=[pltpu.VMEM((t_len, W), F32),
                        pltpu.VMEM((G_GMLP, CHUNK, CHUNK), BF16),
                        pltpu.VMEM((TR, W), BF16)],
        compiler_params=_params(("arbitrary", "arbitrary"), 56),
        name="branch_a",
    )(h3, w_in_bf, ln_g.reshape(1, W), ln_b.reshape(1, W), ws, bias_full)


def _branch_b_kernel(h_ref, w_ref, cw_ref, cb_ref, lng_ref, lnb_ref, o_ref,
                     sc_ref, cv_ref, z0_ref, z1_ref):
    s = pl.program_id(1)
    z_refs = (z0_ref, z1_ref)

    def conv_taps(r0, k_lo, k_hi, init_ref):
        for j in range(N_SLAB):
            cols = pl.ds(j * LANE, LANE)
            for half in range(TR // SHIFT_ROWS):
                rows = pl.ds(r0 + half * SHIFT_ROWS, SHIFT_ROWS)
                base = r0 + half * SHIFT_ROWS + PAD - (CONV_W - 1)
                acc = init_ref[rows, cols] if init_ref is cv_ref else init_ref[:, cols]
                for k in range(k_lo, k_hi):
                    acc = acc + _shifted_rows(sc_ref, j, base + k, SHIFT_ROWS) * cw_ref[k:k + 1, cols]
                cv_ref[rows, cols] = acc

    @pl.when(s == 0)
    def _():
        sc_ref[:, pl.ds(0, PAD), :] = jnp.zeros((N_SLAB, PAD, LANE), F32)

        def keep(r0, z):
            _slab_store(sc_ref, PAD + r0, z)
        _row_tiles(h_ref, w_ref, keep)

    @pl.when(s == 1)
    def _():
        def glu_conv(r0, z_ref):
            gate = jax.nn.sigmoid(z_ref[...])
            for j in range(N_SLAB):
                rows = pl.ds(PAD + r0, TR)
                sc_ref[j, rows, :] = sc_ref[j, rows, :] * gate[:, j * LANE:(j + 1) * LANE]
            conv_taps(r0, 0, CONV_SPLIT, cb_ref)
        _pipelined_row_tiles(h_ref, w_ref, z_refs, glu_conv)

    @pl.when(s == 2)
    def _():
        def norm_gate(r0, z):
            rows = pl.ds(r0, TR)
            conv_taps(r0, CONV_SPLIT, CONV_W, cv_ref)
            y = _silu(_layernorm(cv_ref[rows, :], lng_ref[...], lnb_ref[...]))
            o_ref[rows, :] = (y * _silu(z)).astype(BF16)
        _row_tiles(h_ref, w_ref, norm_gate)


def _branch_b(h3, w_in_bf, conv_w, conv_b, ln_g, ln_b):
    b_sz, t_len, _ = h3.shape
    return pl.pallas_call(
        _branch_b_kernel,
        grid=(b_sz, 3),
        in_specs=[pl.BlockSpec((None, t_len, D), lambda b, s: (b, 0, 0)),
                  pl.BlockSpec((D, W), lambda b, s: (0, 3 + s)),
                  pl.BlockSpec((CONV_W, W), lambda b, s: (0, 0)),
                  pl.BlockSpec((1, W), lambda b, s: (0, 0)),
                  pl.BlockSpec((1, W), lambda b, s: (0, 0)),
                  pl.BlockSpec((1, W), lambda b, s: (0, 0))],
        out_specs=pl.BlockSpec((None, t_len, W), lambda b, s: (b, 0, 0)),
        out_shape=jax.ShapeDtypeStruct((b_sz, t_len, W), BF16),
        scratch_shapes=[pltpu.VMEM((N_SLAB, PAD + t_len, LANE), F32),
                        pltpu.VMEM((t_len, W), F32)] + Z_SCRATCH,
        compiler_params=_params(("arbitrary", "arbitrary"), 60),
        name="branch_b",
    )(h3, w_in_bf, conv_w, conv_b.reshape(1, W), ln_g.reshape(1, W), ln_b.reshape(1, W))


def _branch_c_kernel(h_ref, w_ref, pw_ref, scale_ref, o_ref, sc_ref, yc_ref):
    s = pl.program_id(1)

    @pl.when(s == 0)
    def _():
        sc_ref[:, pl.ds(0, PAD), :] = jnp.zeros((N_SLAB, PAD, LANE), F32)

        def pool(r0, z):
            _slab_store(sc_ref, PAD + r0, z)
            for half in range(TR // SHIFT_ROWS):
                t0 = r0 + half * SHIFT_ROWS
                t1 = (t0 + 1 + lax.broadcasted_iota(jnp.int32, (SHIFT_ROWS, POOL_GW), 0)).astype(F32)
                for g, win in enumerate(POOL_WINDOWS):
                    cols = pl.ds(g * POOL_GW, POOL_GW)
                    slabs = range(g * POOL_GW // LANE, (g + 1) * POOL_GW // LANE)

                    def window(shift):
                        return jnp.concatenate(
                            [_shifted_rows(sc_ref, j, PAD + t0 - shift, SHIFT_ROWS) for j in slabs], axis=1)
                    cur = window(0)
                    wsum = cur
                    for k in range(1, win):
                        wsum = wsum + window(k)
                    diff = wsum / jnp.minimum(t1, float(win)) - cur
                    y = jnp.dot(diff.astype(BF16), pw_ref[g].astype(BF16), preferred_element_type=F32)
                    yc_ref[pl.ds(t0, SHIFT_ROWS), cols] = y * scale_ref[:, cols]
        _row_tiles(h_ref, w_ref, pool)

    @pl.when(s == 1)
    def _():
        def silu_gate(r0, z):
            rows = pl.ds(r0, TR)
            o_ref[rows, :] = (yc_ref[rows, :] * _silu(z)).astype(BF16)
        _row_tiles(h_ref, w_ref, silu_gate)


def _branch_c(h3, w_in_bf, pool_w, pool_scale):
    b_sz, t_len, _ = h3.shape
    n_groups = len(POOL_WINDOWS)
    return pl.pallas_call(
        _branch_c_kernel,
        grid=(b_sz, 2),
        in_specs=[pl.BlockSpec((None, t_len, D), lambda b, s: (b, 0, 0)),
                  pl.BlockSpec((D, W), lambda b, s: (0, 6 + s)),
                  pl.BlockSpec((n_groups, POOL_GW, POOL_GW), lambda b, s: (0, 0, 0)),
                  pl.BlockSpec((1, W), lambda b, s: (0, 0))],
        out_specs=pl.BlockSpec((None, t_len, W), lambda b, s: (b, 0, 0)),
        out_shape=jax.ShapeDtypeStruct((b_sz, t_len, W), BF16),
        scratch_shapes=[pltpu.VMEM((N_SLAB, PAD + t_len, LANE), F32),
                        pltpu.VMEM((t_len, W), F32)],
        compiler_params=_params(("arbitrary", "arbitrary"), 56),
        name="branch_c",
    )(h3, w_in_bf, pool_w, pool_scale.reshape(1, W))


def _branch_m_kernel(h_ref, w_ref, kv_ref, o_ref, sc_ref):
    s = pl.program_id(1)

    @pl.when(s == 0)
    def _():
        def attend(r0, z):
            rows = pl.ds(r0, TR)
            q = (z * (HEAD_DIM ** -0.5)).astype(BF16)
            for hd in range(HEADS):
                cols = pl.ds(hd * HEAD_DIM, HEAD_DIM)
                sc = lax.dot_general(q[:, hd * HEAD_DIM:(hd + 1) * HEAD_DIM], kv_ref[:, cols],
                                     (((1,), (1,)), ((), ())), preferred_element_type=F32)
                e = jnp.exp(sc - jnp.max(sc, axis=-1, keepdims=True))
                p = (e / jnp.sum(e, axis=-1, keepdims=True)).astype(BF16)
                sc_ref[rows, cols] = jnp.dot(
                    p, kv_ref[:, pl.ds(W + hd * HEAD_DIM, HEAD_DIM)], preferred_element_type=F32)
        _row_tiles(h_ref, w_ref, attend)

    @pl.when(s == 1)
    def _():
        def silu_gate(r0, z):
            rows = pl.ds(r0, TR)
            o_ref[rows, :] = (sc_ref[rows, :] * _silu(z)).astype(BF16)
        _row_tiles(h_ref, w_ref, silu_gate)


def _branch_m(h3, w_in_bf, kv, l, mem_len):
    b_sz, t_len, _ = h3.shape
    return pl.pallas_call(
        _branch_m_kernel,
        grid=(b_sz, 2),
        in_specs=[pl.BlockSpec((None, t_len, D), lambda b, s: (b, 0, 0)),
                  pl.BlockSpec((D, W), lambda b, s: (0, 8 + s)),
                  pl.BlockSpec((None, mem_len, 2 * W), lambda b, s: (l, b, 0))],
        out_specs=pl.BlockSpec((None, t_len, W), lambda b, s: (b, 0, 0)),
        out_shape=jax.ShapeDtypeStruct((b_sz, t_len, W), BF16),
        scratch_shapes=[pltpu.VMEM((t_len, W), F32)],
        compiler_params=_params(("arbitrary", "arbitrary"), 56),
        name="branch_m",
    )(h3, w_in_bf, kv)


def _merge_kernel(*refs, n_cast):
    h_ref, y_refs, wg_refs, wb_refs = refs[0], refs[1:5], refs[5:9], refs[9:13]
    cast_in, o_ref, cast_out = refs[13:13 + n_cast], refs[13 + n_cast], refs[14 + n_cast:]
    h = h_ref[...]
    acc = None
    for y_ref, wg, wb in zip(y_refs, wg_refs, wb_refs):
        gate = jax.nn.sigmoid(jnp.dot(h, wg[...], preferred_element_type=F32))
        term = gate * jnp.dot(y_ref[...], wb[...], preferred_element_type=F32)
        acc = term if acc is None else acc + term
    o_ref[...] = acc.astype(BF16)
    for src_ref, dst_ref in zip(cast_in, cast_out):
        dst_ref[...] = src_ref[...].astype(BF16)


def _merge(h2, ys, w_in_bf, w_branch_bf, next_weights_f32, l_next, tm=1024, cn=256):
    n = h2.shape[0]
    tm = min(tm, n)
    n_col = D // cn
    n_steps = (n // tm) * n_col
    gate_blk0 = GATE_COL0 // cn

    def wg_spec(nb):
        return pl.BlockSpec((D, cn), lambda i, c: (0, gate_blk0 + nb * n_col + c))

    def wb_spec(nb):
        return pl.BlockSpec((W, cn), lambda i, c: (nb, c))

    cast_in_specs, cast_out_specs, cast_out_shapes = [], [], []
    for w in next_weights_f32:
        rows, cols = w.shape[1:]
        assert rows % (n_steps * BF16_TILE_ROWS) == 0
        cast_in_specs.append(pl.BlockSpec((None, rows // n_steps, cols), lambda i, c: (l_next, i * n_col + c, 0)))
        cast_out_specs.append(pl.BlockSpec((rows // n_steps, cols), lambda i, c: (i * n_col + c, 0)))
        cast_out_shapes.append(jax.ShapeDtypeStruct((rows, cols), BF16))

    y_spec = pl.BlockSpec((tm, W), lambda i, c: (i, 0))
    outs = pl.pallas_call(
        functools.partial(_merge_kernel, n_cast=len(next_weights_f32)),
        grid=(n // tm, n_col),
        in_specs=[pl.BlockSpec((tm, D), lambda i, c: (i, 0))] + [y_spec] * N_BRANCH
                 + [wg_spec(nb) for nb in range(N_BRANCH)]
                 + [wb_spec(nb) for nb in range(N_BRANCH)] + cast_in_specs,
        out_specs=[pl.BlockSpec((tm, cn), lambda i, c: (i, c))] + cast_out_specs,
        out_shape=[jax.ShapeDtypeStruct((n, D), BF16)] + cast_out_shapes,
        compiler_params=_params(("arbitrary", "arbitrary"), 56),
        name="gate_merge",
    )(h2, *ys, *([w_in_bf] * N_BRANCH), *([w_branch_bf] * N_BRANCH), *next_weights_f32)
    return outs[0], tuple(outs[1:])


def _outproj_kernel(x_ref, m_ref, w_ref, g_ref, *out_refs, sub, last):
    def body(r, carry):
        rows = pl.ds(pl.multiple_of(r * sub, sub), sub)
        xn = x_ref[rows, :] + jnp.dot(m_ref[rows, :], w_ref[...], preferred_element_type=F32)
        normed = _rms(xn, g_ref[...])
        if last:
            out_refs[0][rows, :] = normed
        else:
            out_refs[0][rows, :] = xn
            out_refs[1][rows, :] = normed.astype(BF16)
        return carry
    lax.fori_loop(0, x_ref.shape[0] // sub, body, 0, unroll=2)


def _outproj(x2, merged, w_out_bf, g_next, last, tm=512, sub=256):
    n = x2.shape[0]
    tm = min(tm, n)
    row_spec = pl.BlockSpec((tm, D), lambda i: (i, 0))
    if last:
        out_specs, out_shape = row_spec, jax.ShapeDtypeStruct((n, D), F32)
    else:
        out_specs = [row_spec, row_spec]
        out_shape = [jax.ShapeDtypeStruct((n, D), F32), jax.ShapeDtypeStruct((n, D), BF16)]
    return pl.pallas_call(
        functools.partial(_outproj_kernel, sub=min(sub, tm), last=last),
        grid=(n // tm,),
        in_specs=[row_spec, row_spec,
                  pl.BlockSpec((D, D), lambda i: (0, 0)),
                  pl.BlockSpec((1, D), lambda i: (0, 0))],
        out_specs=out_specs,
        out_shape=out_shape,
        compiler_params=_params(("arbitrary",), 56),
        name="out_proj",
    )(x2, merged, w_out_bf, g_next.reshape(1, D))


def kernel(x, mem, norm_g, w_in, gmlp_ln_g, gmlp_ln_b, gmlp_ws, gmlp_bs, conv_w, conv_b,
           conv_ln_g, conv_ln_b, pool_w, pool_scale, mem_norm_g, w_kv, w_branch, w_out, final_g):
    b_sz, t_len, d = x.shape
    n_layers = w_in.shape[0]
    mem_len = mem.shape[1]
    assert d == D and w_in.shape[2] == GATE_COL0 + N_BRANCH * D
    assert t_len % (2 * TR) == 0 and TR % CHUNK == 0

    w_branch2 = w_branch.reshape(n_layers, N_BRANCH * W, D)
    layer_w = (w_in[0].astype(BF16), w_branch2[0].astype(BF16), w_out[0].astype(BF16))
    w_kv_bf = w_kv.astype(BF16)

    n = b_sz * t_len
    x2 = x.reshape(n, D)
    kv = _memory_kv(mem.reshape(b_sz * mem_len, D), mem_norm_g, w_kv_bf)
    h2 = _rmsnorm(x2, norm_g[0], BF16)

    for l in range(n_layers):
        w_in_bf, w_branch_bf, w_out_bf = layer_w
        last = l == n_layers - 1
        h3 = h2.reshape(b_sz, t_len, D)
        bias_full = jnp.repeat(gmlp_bs[l].T, LANE, axis=1)
        ys = (
            _branch_a(h3, w_in_bf, gmlp_ln_g[l], gmlp_ln_b[l], gmlp_ws[l], bias_full),
            _branch_b(h3, w_in_bf, conv_w[l], conv_b[l], conv_ln_g[l], conv_ln_b[l]),
            _branch_c(h3, w_in_bf, pool_w[l], pool_scale[l]),
            _branch_m(h3, w_in_bf, kv, l, mem_len),
        )
        ys = tuple(y.reshape(n, W) for y in ys)
        next_f32 = () if last else (w_in, w_branch2, w_out)
        merged, layer_w = _merge(h2, ys, w_in_bf, w_branch_bf, next_f32, l + 1)
        g_next = final_g if last else norm_g[l + 1]
        res = _outproj(x2, merged, w_out_bf, g_next, last)
        if last:
            return res.reshape(b_sz, t_len, D)
        x2, h2 = res
```

```python
import functools

import jax
import jax.numpy as jnp
from jax import lax
from jax.experimental import pallas as pl
from jax.experimental.pallas import tpu as pltpu

F32 = jnp.float32
BF16 = jnp.bfloat16

D = 2048
W = 1024
CHUNK = 128
G_GMLP = 8
CONV_W = 31
POOL_WINDOWS = (2, 4, 8, 16)
POOL_GW = 256
HEADS = 4
HEAD_DIM = 256
EPS = 1e-6
N_BRANCH = 4
LANE = 128
BF16_TILE_ROWS = 16

PAD = 32
SHIFT_ROWS = 128
CONV_SPLIT = 19
TR = 256
GATE_COL0 = 10 * W

MIB = 1024 * 1024


def _params(sem, vmem_mib):
    return pltpu.CompilerParams(dimension_semantics=sem, vmem_limit_bytes=vmem_mib * MIB)


def _layernorm(x, g, b):
    mu = jnp.mean(x, axis=-1, keepdims=True)
    xc = x - mu
    var = jnp.mean(xc * xc, axis=-1, keepdims=True)
    return xc * lax.rsqrt(var + EPS) * g + b


def _rms(x, g):
    return x * lax.rsqrt(jnp.mean(x * x, axis=-1, keepdims=True) + EPS) * g


def _silu(x):
    return x * jax.nn.sigmoid(x)


def _row_tiles(h_ref, w_ref, epilogue, unroll=None):
    n_tiles = h_ref.shape[0] // TR

    def step(r, carry):
        r0 = r * TR if isinstance(r, int) else pl.multiple_of(r * TR, TR)
        epilogue(r0, jnp.dot(h_ref[pl.ds(r0, TR), :], w_ref[...], preferred_element_type=F32))
        return carry
    if unroll is None:
        for r in range(n_tiles):
            step(r, 0)
    else:
        lax.fori_loop(0, n_tiles, step, 0, unroll=unroll)


def _pipelined_row_tiles(h_ref, w_ref, z_refs, epilogue):
    t_len = h_ref.shape[0]
    n_tiles = t_len // TR
    z0_ref, z1_ref = z_refs

    def project(r0, z_ref):
        z_ref[...] = jnp.dot(h_ref[pl.ds(r0, TR), :], w_ref[...], preferred_element_type=F32)

    project(0, z0_ref)

    def step(i, carry):
        r0 = pl.multiple_of(i * (2 * TR), 2 * TR)
        project(r0 + TR, z1_ref)
        epilogue(r0, z0_ref)
        project(r0 + 2 * TR, z0_ref)
        epilogue(r0 + TR, z1_ref)
        return carry
    lax.fori_loop(0, n_tiles // 2 - 1, step, 0)
    r0 = t_len - 2 * TR
    project(r0 + TR, z1_ref)
    epilogue(r0, z0_ref)
    epilogue(r0 + TR, z1_ref)


Z_SCRATCH = [pltpu.VMEM((TR, W), F32), pltpu.VMEM((TR, W), F32)]
N_SLAB = W // LANE


def _slab_store(ref, start, value):
    for j in range(N_SLAB):
        ref[j, pl.ds(start, value.shape[0]), :] = value[:, j * LANE:(j + 1) * LANE]


def _shifted_rows(ref, slab, start, n_rows):
    return ref[slab, pl.ds(start, n_rows, stride=1), :]


def _rmsnorm_kernel(x_ref, g_ref, o_ref):
    o_ref[...] = _rms(x_ref[...], g_ref[...]).astype(o_ref.dtype)


def _rmsnorm(x2d, g, out_dtype, tm=512):
    n = x2d.shape[0]
    return pl.pallas_call(
        _rmsnorm_kernel,
        grid=(n // tm,),
        in_specs=[pl.BlockSpec((tm, D), lambda i: (i, 0)),
                  pl.BlockSpec((1, D), lambda i: (0, 0))],
        out_specs=pl.BlockSpec((tm, D), lambda i: (i, 0)),
        out_shape=jax.ShapeDtypeStruct((n, D), out_dtype),
        compiler_params=_params(("arbitrary",), 40),
        name="rmsnorm",
    )(x2d, g.reshape(1, D))


def _kv_kernel(m_ref, g_ref, w_ref, o_ref):
    mn = _rms(m_ref[...], g_ref[...]).astype(BF16)
    o_ref[...] = jnp.dot(mn, w_ref[...], preferred_element_type=F32).astype(BF16)


def _memory_kv(mem2d, mem_norm_g, w_kv_bf, tm=512):
    n_layers = w_kv_bf.shape[0]
    n = mem2d.shape[0]
    tm = min(tm, n)
    return pl.pallas_call(
        _kv_kernel,
        grid=(n_layers, n // tm),
        in_specs=[pl.BlockSpec((tm, D), lambda l, i: (i, 0)),
                  pl.BlockSpec((None, 1, D), lambda l, i: (l, 0, 0)),
                  pl.BlockSpec((None, D, 2 * W), lambda l, i: (l, 0, 0))],
        out_specs=pl.BlockSpec((None, tm, 2 * W), lambda l, i: (l, i, 0)),
        out_shape=jax.ShapeDtypeStruct((n_layers, n, 2 * W), BF16),
        compiler_params=_params(("arbitrary", "arbitrary"), 48),
        name="memory_kv",
    )(mem2d, mem_norm_g.reshape(n_layers, 1, D), w_kv_bf)


def _branch_a_kernel(h_ref, w_ref, lng_ref, lnb_ref, ws_ref, bias_ref, o_ref,
                     sc_ref, wsm_ref, vln_ref):
    s = pl.program_id(1)

    @pl.when(s == 0)
    def _():
        row = lax.broadcasted_iota(jnp.int32, (CHUNK, CHUNK), 0)
        col = lax.broadcasted_iota(jnp.int32, (CHUNK, CHUNK), 1)
        for g in range(G_GMLP):
            wsm_ref[g] = jnp.where(row >= col, ws_ref[g], 0.0).astype(BF16)

        def gelu_u(r0, z):
            sc_ref[pl.ds(r0, TR), :] = jax.nn.gelu(z)
        _row_tiles(h_ref, w_ref, gelu_u)

    @pl.when(s == 1)
    def _():
        def spatial_gate(r0, z):
            v = _layernorm(jax.nn.gelu(z), lng_ref[...], lnb_ref[...])
            vln_ref[...] = v.astype(BF16)
            for c in range(TR // CHUNK):
                for g in range(G_GMLP):
                    rows = pl.ds(r0 + c * CHUNK, CHUNK)
                    cols = pl.ds(g * LANE, LANE)
                    sp = jnp.dot(wsm_ref[g], vln_ref[pl.ds(c * CHUNK, CHUNK), cols],
                                 preferred_element_type=F32) + bias_ref[:, cols]
                    sc_ref[rows, cols] = sc_ref[rows, cols] * sp
        _row_tiles(h_ref, w_ref, spatial_gate)

    @pl.when(s == 2)
    def _():
        def silu_gate(r0, z):
            rows = pl.ds(r0, TR)
            o_ref[rows, :] = (sc_ref[rows, :] * _silu(z)).astype(BF16)
        _row_tiles(h_ref, w_ref, silu_gate)


def _branch_a(h3, w_in_bf, ln_g, ln_b, ws, bias_full):
    b_sz, t_len, _ = h3.shape
    return pl.pallas_call(
        _branch_a_kernel,
        grid=(b_sz, 3),
        in_specs=[pl.BlockSpec((None, t_len, D), lambda b, s: (b, 0, 0)),
                  pl.BlockSpec((D, W), lambda b, s: (0, s)),
                  pl.BlockSpec((1, W), lambda b, s: (0, 0)),
                  pl.BlockSpec((1, W), lambda b, s: (0, 0)),
                  pl.BlockSpec((G_GMLP, CHUNK, CHUNK), lambda b, s: (0, 0, 0)),
                  pl.BlockSpec((CHUNK, W), lambda b, s: (0, 0))],
        out_specs=pl.BlockSpec((None, t_len, W), lambda b, s: (b, 0, 0)),
        out_shape=jax.ShapeDtypeStruct((b_sz, t_len, W), BF16),
        scratch_shapes=[pltpu.VMEM((t_len, W), F32),
                        pltpu.VMEM((G_GMLP, CHUNK, CHUNK), BF16),
                        pltpu.VMEM((TR, W), BF16)],
        compiler_params=_params(("arbitrary", "arbitrary"), 56),
        name="branch_a",
    )(h3, w_in_bf, ln_g.reshape(1, W), ln_b.reshape(1, W), ws, bias_full)


def _branch_b_kernel(h_ref, w_ref, cw_ref, cb_ref, lng_ref, lnb_ref, o_ref,
                     sc_ref, cv_ref, z0_ref, z1_ref):
    s = pl.program_id(1)
    z_refs = (z0_ref, z1_ref)

    def conv_taps(r0, k_lo, k_hi, init_ref):
        for j in range(N_SLAB):
            cols = pl.ds(j * LANE, LANE)
            for half in range(TR // SHIFT_ROWS):
                rows = pl.ds(r0 + half * SHIFT_ROWS, SHIFT_ROWS)
                base = r0 + half * SHIFT_ROWS + PAD - (CONV_W - 1)
                acc = init_ref[rows, cols] if init_ref is cv_ref else init_ref[:, cols]
                for k in range(k_lo, k_hi):
                    acc = acc + _shifted_rows(sc_ref, j, base + k, SHIFT_ROWS) * cw_ref[k:k + 1, cols]
                cv_ref[rows, cols] = acc

    @pl.when(s == 0)
    def _():
        sc_ref[:, pl.ds(0, PAD), :] = jnp.zeros((N_SLAB, PAD, LANE), F32)

        def keep(r0, z):
            _slab_store(sc_ref, PAD + r0, z)
        _row_tiles(h_ref, w_ref, keep, unroll=2)

    @pl.when(s == 1)
    def _():
        def glu_conv(r0, z_ref):
            gate = jax.nn.sigmoid(z_ref[...])
            for j in range(N_SLAB):
                rows = pl.ds(PAD + r0, TR)
                sc_ref[j, rows, :] = sc_ref[j, rows, :] * gate[:, j * LANE:(j + 1) * LANE]
            conv_taps(r0, 0, CONV_SPLIT, cb_ref)
        _pipelined_row_tiles(h_ref, w_ref, z_refs, glu_conv)

    @pl.when(s == 2)
    def _():
        def norm_gate(r0, z):
            rows = pl.ds(r0, TR)
            conv_taps(r0, CONV_SPLIT, CONV_W, cv_ref)
            y = _silu(_layernorm(cv_ref[rows, :], lng_ref[...], lnb_ref[...]))
            o_ref[rows, :] = (y * _silu(z)).astype(BF16)
        _row_tiles(h_ref, w_ref, norm_gate, unroll=2)


def _branch_b(h3, w_in_bf, conv_w, conv_b, ln_g, ln_b):
    b_sz, t_len, _ = h3.shape
    return pl.pallas_call(
        _branch_b_kernel,
        grid=(b_sz, 3),
        in_specs=[pl.BlockSpec((None, t_len, D), lambda b, s: (b, 0, 0)),
                  pl.BlockSpec((D, W), lambda b, s: (0, 3 + s)),
                  pl.BlockSpec((CONV_W, W), lambda b, s: (0, 0)),
                  pl.BlockSpec((1, W), lambda b, s: (0, 0)),
                  pl.BlockSpec((1, W), lambda b, s: (0, 0)),
                  pl.BlockSpec((1, W), lambda b, s: (0, 0))],
        out_specs=pl.BlockSpec((None, t_len, W), lambda b, s: (b, 0, 0)),
        out_shape=jax.ShapeDtypeStruct((b_sz, t_len, W), BF16),
        scratch_shapes=[pltpu.VMEM((N_SLAB, PAD + t_len, LANE), F32),
                        pltpu.VMEM((t_len, W), F32)] + Z_SCRATCH,
        compiler_params=_params(("arbitrary", "arbitrary"), 60),
        name="branch_b",
    )(h3, w_in_bf, conv_w, conv_b.reshape(1, W), ln_g.reshape(1, W), ln_b.reshape(1, W))


def _branch_c_kernel(h_ref, w_ref, pw_ref, scale_ref, o_ref, sc_ref, yc_ref):
    s = pl.program_id(1)

    @pl.when(s == 0)
    def _():
        sc_ref[:, pl.ds(0, PAD), :] = jnp.zeros((N_SLAB, PAD, LANE), F32)

        def pool(r0, z):
            _slab_store(sc_ref, PAD + r0, z)
            for half in range(TR // SHIFT_ROWS):
                t0 = r0 + half * SHIFT_ROWS
                t1 = (t0 + 1 + lax.broadcasted_iota(jnp.int32, (SHIFT_ROWS, POOL_GW), 0)).astype(F32)
                for g, win in enumerate(POOL_WINDOWS):
                    cols = pl.ds(g * POOL_GW, POOL_GW)
                    slabs = range(g * POOL_GW // LANE, (g + 1) * POOL_GW // LANE)

                    def window(shift):
                        return jnp.concatenate(
                            [_shifted_rows(sc_ref, j, PAD + t0 - shift, SHIFT_ROWS) for j in slabs], axis=1)
                    cur = window(0)
                    wsum = cur
                    for k in range(1, win):
                        wsum = wsum + window(k)
                    diff = wsum / jnp.minimum(t1, float(win)) - cur
                    y = jnp.dot(diff.astype(BF16), pw_ref[g].astype(BF16), preferred_element_type=F32)
                    yc_ref[pl.ds(t0, SHIFT_ROWS), cols] = y * scale_ref[:, cols]
        _row_tiles(h_ref, w_ref, pool)

    @pl.when(s == 1)
    def _():
        def silu_gate(r0, z):
            rows = pl.ds(r0, TR)
            o_ref[rows, :] = (yc_ref[rows, :] * _silu(z)).astype(BF16)
        _row_tiles(h_ref, w_ref, silu_gate)


def _branch_c(h3, w_in_bf, pool_w, pool_scale):
    b_sz, t_len, _ = h3.shape
    n_groups = len(POOL_WINDOWS)
    return pl.pallas_call(
        _branch_c_kernel,
        grid=(b_sz, 2),
        in_specs=[pl.BlockSpec((None, t_len, D), lambda b, s: (b, 0, 0)),
                  pl.BlockSpec((D, W), lambda b, s: (0, 6 + s)),
                  pl.BlockSpec((n_groups, POOL_GW, POOL_GW), lambda b, s: (0, 0, 0)),
                  pl.BlockSpec((1, W), lambda b, s: (0, 0))],
        out_specs=pl.BlockSpec((None, t_len, W), lambda b, s: (b, 0, 0)),
        out_shape=jax.ShapeDtypeStruct((b_sz, t_len, W), BF16),
        scratch_shapes=[pltpu.VMEM((N_SLAB, PAD + t_len, LANE), F32),
                        pltpu.VMEM((t_len, W), F32)],
        compiler_params=_params(("arbitrary", "arbitrary"), 56),
        name="branch_c",
    )(h3, w_in_bf, pool_w, pool_scale.reshape(1, W))


def _branch_m_kernel(h_ref, w_ref, kv_ref, o_ref, sc_ref):
    s = pl.program_id(1)

    @pl.when(s == 0)
    def _():
        def attend(r0, z):
            rows = pl.ds(r0, TR)
            q = (z * (HEAD_DIM ** -0.5)).astype(BF16)
            for hd in range(HEADS):
                cols = pl.ds(hd * HEAD_DIM, HEAD_DIM)
                sc = lax.dot_general(q[:, hd * HEAD_DIM:(hd + 1) * HEAD_DIM], kv_ref[:, cols],
                                     (((1,), (1,)), ((), ())), preferred_element_type=F32)
                e = jnp.exp(sc - jnp.max(sc, axis=-1, keepdims=True))
                p = (e / jnp.sum(e, axis=-1, keepdims=True)).astype(BF16)
                sc_ref[rows, cols] = jnp.dot(
                    p, kv_ref[:, pl.ds(W + hd * HEAD_DIM, HEAD_DIM)], preferred_element_type=F32)
        _row_tiles(h_ref, w_ref, attend)

    @pl.when(s == 1)
    def _():
        def silu_gate(r0, z):
            rows = pl.ds(r0, TR)
            o_ref[rows, :] = (sc_ref[rows, :] * _silu(z)).astype(BF16)
        _row_tiles(h_ref, w_ref, silu_gate)


def _branch_m(h3, w_in_bf, kv, l, mem_len):
    b_sz, t_len, _ = h3.shape
    return pl.pallas_call(
        _branch_m_kernel,
        grid=(b_sz, 2),
        in_specs=[pl.BlockSpec((None, t_len, D), lambda b, s: (b, 0, 0)),
                  pl.BlockSpec((D, W), lambda b, s: (0, 8 + s)),
                  pl.BlockSpec((None, mem_len, 2 * W), lambda b, s: (l, b, 0))],
        out_specs=pl.BlockSpec((None, t_len, W), lambda b, s: (b, 0, 0)),
        out_shape=jax.ShapeDtypeStruct((b_sz, t_len, W), BF16),
        scratch_shapes=[pltpu.VMEM((t_len, W), F32)],
        compiler_params=_params(("arbitrary", "arbitrary"), 56),
        name="branch_m",
    )(h3, w_in_bf, kv)


def _merge_kernel(*refs, n_cast):
    h_ref, y_refs, wg_refs, wb_refs = refs[0], refs[1:5], refs[5:9], refs[9:13]
    cast_in, o_ref, cast_out = refs[13:13 + n_cast], refs[13 + n_cast], refs[14 + n_cast:]
    h = h_ref[...]
    acc = None
    for y_ref, wg, wb in zip(y_refs, wg_refs, wb_refs):
        gate = jax.nn.sigmoid(jnp.dot(h, wg[...], preferred_element_type=F32))
        term = gate * jnp.dot(y_ref[...], wb[...], preferred_element_type=F32)
        acc = term if acc is None else acc + term
    o_ref[...] = acc.astype(BF16)
    for src_ref, dst_ref in zip(cast_in, cast_out):
        dst_ref[...] = src_ref[...].astype(BF16)


def _merge(h2, ys, w_in_bf, w_branch_bf, next_weights_f32, l_next, tm=1024, cn=256):
    n = h2.shape[0]
    tm = min(tm, n)
    n_col = D // cn
    n_steps = (n // tm) * n_col
    gate_blk0 = GATE_COL0 // cn

    def wg_spec(nb):
        return pl.BlockSpec((D, cn), lambda i, c: (0, gate_blk0 + nb * n_col + c))

    def wb_spec(nb):
        return pl.BlockSpec((W, cn), lambda i, c: (nb, c))

    cast_in_specs, cast_out_specs, cast_out_shapes = [], [], []
    for w in next_weights_f32:
        rows, cols = w.shape[1:]
        assert rows % (n_steps * BF16_TILE_ROWS) == 0
        cast_in_specs.append(pl.BlockSpec((None, rows // n_steps, cols), lambda i, c: (l_next, i * n_col + c, 0)))
        cast_out_specs.append(pl.BlockSpec((rows // n_steps, cols), lambda i, c: (i * n_col + c, 0)))
        cast_out_shapes.append(jax.ShapeDtypeStruct((rows, cols), BF16))

    y_spec = pl.BlockSpec((tm, W), lambda i, c: (i, 0))
    outs = pl.pallas_call(
        functools.partial(_merge_kernel, n_cast=len(next_weights_f32)),
        grid=(n // tm, n_col),
        in_specs=[pl.BlockSpec((tm, D), lambda i, c: (i, 0))] + [y_spec] * N_BRANCH
                 + [wg_spec(nb) for nb in range(N_BRANCH)]
                 + [wb_spec(nb) for nb in range(N_BRANCH)] + cast_in_specs,
        out_specs=[pl.BlockSpec((tm, cn), lambda i, c: (i, c))] + cast_out_specs,
        out_shape=[jax.ShapeDtypeStruct((n, D), BF16)] + cast_out_shapes,
        compiler_params=_params(("arbitrary", "arbitrary"), 56),
        name="gate_merge",
    )(h2, *ys, *([w_in_bf] * N_BRANCH), *([w_branch_bf] * N_BRANCH), *next_weights_f32)
    return outs[0], tuple(outs[1:])


def _outproj_kernel(x_ref, m_ref, w_ref, g_ref, *out_refs, sub, last):
    def body(r, carry):
        rows = pl.ds(pl.multiple_of(r * sub, sub), sub)
        xn = x_ref[rows, :] + jnp.dot(m_ref[rows, :], w_ref[...], preferred_element_type=F32)
        normed = _rms(xn, g_ref[...])
        if last:
            out_refs[0][rows, :] = normed
        else:
            out_refs[0][rows, :] = xn
            out_refs[1][rows, :] = normed.astype(BF16)
        return carry
    lax.fori_loop(0, x_ref.shape[0] // sub, body, 0, unroll=2)


def _outproj(x2, merged, w_out_bf, g_next, last, tm=512, sub=256):
    n = x2.shape[0]
    tm = min(tm, n)
    row_spec = pl.BlockSpec((tm, D), lambda i: (i, 0))
    if last:
        out_specs, out_shape = row_spec, jax.ShapeDtypeStruct((n, D), F32)
    else:
        out_specs = [row_spec, row_spec]
        out_shape = [jax.ShapeDtypeStruct((n, D), F32), jax.ShapeDtypeStruct((n, D), BF16)]
    return pl.pallas_call(
        functools.partial(_outproj_kernel, sub=min(sub, tm), last=last),
        grid=(n // tm,),
        in_specs=[row_spec, row_spec,
                  pl.BlockSpec((D, D), lambda i: (0, 0)),
                  pl.BlockSpec((1, D), lambda i: (0, 0))],
        out_specs=out_specs,
        out_shape=out_shape,
        compiler_params=_params(("arbitrary",), 56),
        name="out_proj",
    )(x2, merged, w_out_bf, g_next.reshape(1, D))


def kernel(x, mem, norm_g, w_in, gmlp_ln_g, gmlp_ln_b, gmlp_ws, gmlp_bs, conv_w, conv_b,
           conv_ln_g, conv_ln_b, pool_w, pool_scale, mem_norm_g, w_kv, w_branch, w_out, final_g):
    b_sz, t_len, d = x.shape
    n_layers = w_in.shape[0]
    mem_len = mem.shape[1]
    assert d == D and w_in.shape[2] == GATE_COL0 + N_BRANCH * D
    assert t_len % (2 * TR) == 0 and TR % CHUNK == 0

    w_branch2 = w_branch.reshape(n_layers, N_BRANCH * W, D)
    layer_w = (w_in[0].astype(BF16), w_branch2[0].astype(BF16), w_out[0].astype(BF16))
    w_kv_bf = w_kv.astype(BF16)

    n = b_sz * t_len
    x2 = x.reshape(n, D)
    kv = _memory_kv(mem.reshape(b_sz * mem_len, D), mem_norm_g, w_kv_bf)
    h2 = _rmsnorm(x2, norm_g[0], BF16)

    for l in range(n_layers):
        w_in_bf, w_branch_bf, w_out_bf = layer_w
        last = l == n_layers - 1
        h3 = h2.reshape(b_sz, t_len, D)
        bias_full = jnp.repeat(gmlp_bs[l].T, LANE, axis=1)
        ys = (
            _branch_a(h3, w_in_bf, gmlp_ln_g[l], gmlp_ln_b[l], gmlp_ws[l], bias_full),
            _branch_b(h3, w_in_bf, conv_w[l], conv_b[l], conv_ln_g[l], conv_ln_b[l]),
            _branch_c(h3, w_in_bf, pool_w[l], pool_scale[l]),
            _branch_m(h3, w_in_bf, kv, l, mem_len),
        )
        ys = tuple(y.reshape(n, W) for y in ys)
        next_f32 = () if last else (w_in, w_branch2, w_out)
        merged, layer_w = _merge(h2, ys, w_in_bf, w_branch_bf, next_f32, l + 1)
        g_next = final_g if last else norm_g[l + 1]
        res = _outproj(x2, merged, w_out_bf, g_next, last)
        if last:
            return res.reshape(b_sz, t_len, D)
        x2, h2 = res
```

```python
import functools

import jax
import jax.numpy as jnp
from jax import lax
from jax.experimental import pallas as pl
from jax.experimental.pallas import tpu as pltpu

F32 = jnp.float32
BF16 = jnp.bfloat16

D = 2048
W = 1024
CHUNK = 128
G_GMLP = 8
CONV_W = 31
POOL_WINDOWS = (2, 4, 8, 16)
POOL_GW = 256
HEADS = 4
HEAD_DIM = 256
EPS = 1e-6
N_BRANCH = 4
LANE = 128
BF16_TILE_ROWS = 16

PAD = 32
SHIFT_ROWS = 128
CONV_ROWS = 64
CONV_SPLIT = 19
TR = 256
GATE_COL0 = 10 * W

MIB = 1024 * 1024


def _params(sem, vmem_mib):
    return pltpu.CompilerParams(dimension_semantics=sem, vmem_limit_bytes=vmem_mib * MIB)


def _layernorm(x, g, b):
    mu = jnp.mean(x, axis=-1, keepdims=True)
    xc = x - mu
    var = jnp.mean(xc * xc, axis=-1, keepdims=True)
    return xc * lax.rsqrt(var + EPS) * g + b


def _rms(x, g):
    return x * lax.rsqrt(jnp.mean(x * x, axis=-1, keepdims=True) + EPS) * g


def _silu(x):
    return x * jax.nn.sigmoid(x)


def _row_tiles(h_ref, w_ref, epilogue, unroll=None):
    n_tiles = h_ref.shape[0] // TR

    def step(r, carry):
        r0 = r * TR if isinstance(r, int) else pl.multiple_of(r * TR, TR)
        epilogue(r0, jnp.dot(h_ref[pl.ds(r0, TR), :], w_ref[...], preferred_element_type=F32))
        return carry
    if unroll is None:
        for r in range(n_tiles):
            step(r, 0)
    else:
        lax.fori_loop(0, n_tiles, step, 0, unroll=unroll)


def _pipelined_row_tiles(h_ref, w_ref, z_refs, epilogue):
    t_len = h_ref.shape[0]
    n_tiles = t_len // TR
    z0_ref, z1_ref = z_refs

    def project(r0, z_ref):
        z_ref[...] = jnp.dot(h_ref[pl.ds(r0, TR), :], w_ref[...], preferred_element_type=F32)

    project(0, z0_ref)

    def step(i, carry):
        r0 = pl.multiple_of(i * (2 * TR), 2 * TR)
        project(r0 + TR, z1_ref)
        epilogue(r0, z0_ref)
        project(r0 + 2 * TR, z0_ref)
        epilogue(r0 + TR, z1_ref)
        return carry
    lax.fori_loop(0, n_tiles // 2 - 1, step, 0)
    r0 = t_len - 2 * TR
    project(r0 + TR, z1_ref)
    epilogue(r0, z0_ref)
    epilogue(r0 + TR, z1_ref)


Z_SCRATCH = [pltpu.VMEM((TR, W), F32), pltpu.VMEM((TR, W), F32)]
N_SLAB = W // LANE


def _slab_store(ref, start, value):
    for j in range(N_SLAB):
        ref[j, pl.ds(start, value.shape[0]), :] = value[:, j * LANE:(j + 1) * LANE]


def _shifted_rows(ref, slab, start, n_rows):
    return ref[slab, pl.ds(start, n_rows, stride=1), :]


def _rmsnorm_kernel(x_ref, g_ref, o_ref):
    o_ref[...] = _rms(x_ref[...], g_ref[...]).astype(o_ref.dtype)


def _rmsnorm(x2d, g, out_dtype, tm=512):
    n = x2d.shape[0]
    return pl.pallas_call(
        _rmsnorm_kernel,
        grid=(n // tm,),
        in_specs=[pl.BlockSpec((tm, D), lambda i: (i, 0)),
                  pl.BlockSpec((1, D), lambda i: (0, 0))],
        out_specs=pl.BlockSpec((tm, D), lambda i: (i, 0)),
        out_shape=jax.ShapeDtypeStruct((n, D), out_dtype),
        compiler_params=_params(("arbitrary",), 40),
        name="rmsnorm",
    )(x2d, g.reshape(1, D))


def _kv_kernel(m_ref, g_ref, w_ref, o_ref):
    mn = _rms(m_ref[...], g_ref[...]).astype(BF16)
    o_ref[...] = jnp.dot(mn, w_ref[...], preferred_element_type=F32).astype(BF16)


def _memory_kv(mem2d, mem_norm_g, w_kv_bf, tm=512):
    n_layers = w_kv_bf.shape[0]
    n = mem2d.shape[0]
    tm = min(tm, n)
    return pl.pallas_call(
        _kv_kernel,
        grid=(n_layers, n // tm),
        in_specs=[pl.BlockSpec((tm, D), lambda l, i: (i, 0)),
                  pl.BlockSpec((None, 1, D), lambda l, i: (l, 0, 0)),
                  pl.BlockSpec((None, D, 2 * W), lambda l, i: (l, 0, 0))],
        out_specs=pl.BlockSpec((None, tm, 2 * W), lambda l, i: (l, i, 0)),
        out_shape=jax.ShapeDtypeStruct((n_layers, n, 2 * W), BF16),
        compiler_params=_params(("arbitrary", "arbitrary"), 48),
        name="memory_kv",
    )(mem2d, mem_norm_g.reshape(n_layers, 1, D), w_kv_bf)


def _branch_a_kernel(h_ref, w_ref, lng_ref, lnb_ref, ws_ref, bias_ref, o_ref,
                     sc_ref, wsm_ref, vln_ref):
    s = pl.program_id(1)

    @pl.when(s == 0)
    def _():
        row = lax.broadcasted_iota(jnp.int32, (CHUNK, CHUNK), 0)
        col = lax.broadcasted_iota(jnp.int32, (CHUNK, CHUNK), 1)
        for g in range(G_GMLP):
            wsm_ref[g] = jnp.where(row >= col, ws_ref[g], 0.0).astype(BF16)

        def gelu_u(r0, z):
            sc_ref[pl.ds(r0, TR), :] = jax.nn.gelu(z)
        _row_tiles(h_ref, w_ref, gelu_u)

    @pl.when(s == 1)
    def _():
        def spatial_gate(r0, z):
            v = _layernorm(jax.nn.gelu(z), lng_ref[...], lnb_ref[...])
            vln_ref[...] = v.astype(BF16)
            for c in range(TR // CHUNK):
                for g in range(G_GMLP):
                    rows = pl.ds(r0 + c * CHUNK, CHUNK)
                    cols = pl.ds(g * LANE, LANE)
                    sp = jnp.dot(wsm_ref[g], vln_ref[pl.ds(c * CHUNK, CHUNK), cols],
                                 preferred_element_type=F32) + bias_ref[:, cols]
                    sc_ref[rows, cols] = sc_ref[rows, cols] * sp
        _row_tiles(h_ref, w_ref, spatial_gate)

    @pl.when(s == 2)
    def _():
        def silu_gate(r0, z):
            rows = pl.ds(r0, TR)
            o_ref[rows, :] = (sc_ref[rows, :] * _silu(z)).astype(BF16)
        _row_tiles(h_ref, w_ref, silu_gate)


def _branch_a(h3, w_in_bf, ln_g, ln_b, ws, bias_full):
    b_sz, t_len, _ = h3.shape
    return pl.pallas_call(
        _branch_a_kernel,
        grid=(b_sz, 3),
        in_specs=[pl.BlockSpec((None, t_len, D), lambda b, s: (b, 0, 0)),
                  pl.BlockSpec((D, W), lambda b, s: (0, s)),
                  pl.BlockSpec((1, W), lambda b, s: (0, 0)),
                  pl.BlockSpec((1, W), lambda b, s: (0, 0)),
                  pl.BlockSpec((G_GMLP, CHUNK, CHUNK), lambda b, s: (0, 0, 0)),
                  pl.BlockSpec((CHUNK, W), lambda b, s: (0, 0))],
        out_specs=pl.BlockSpec((None, t_len, W), lambda b, s: (b, 0, 0)),
        out_shape=jax.ShapeDtypeStruct((b_sz, t_len, W), BF16),
        scratch_shapes=[pltpu.VMEM((t_len, W), F32),
                        pltpu.VMEM((G_GMLP, CHUNK, CHUNK), BF16),
                        pltpu.VMEM((TR, W), BF16)],
        compiler_params=_params(("arbitrary", "arbitrary"), 56),
        name="branch_a",
    )(h3, w_in_bf, ln_g.reshape(1, W), ln_b.reshape(1, W), ws, bias_full)


def _branch_b_kernel(h_ref, w_ref, cw_ref, cb_ref, lng_ref, lnb_ref, o_ref,
                     sc_ref, cv_ref, z0_ref, z1_ref):
    s = pl.program_id(1)
    z_refs = (z0_ref, z1_ref)

    def conv_taps(r0, k_lo, k_hi, init_ref):
        for j in range(N_SLAB):
            cols = pl.ds(j * LANE, LANE)
            for part in range(TR // CONV_ROWS):
                rows = pl.ds(r0 + part * CONV_ROWS, CONV_ROWS)
                base = r0 + part * CONV_ROWS + PAD - (CONV_W - 1)
                acc = init_ref[rows, cols] if init_ref is cv_ref else init_ref[:, cols]
                for k in range(k_lo, k_hi):
                    acc = acc + _shifted_rows(sc_ref, j, base + k, CONV_ROWS) * cw_ref[k:k + 1, cols]
                cv_ref[rows, cols] = acc

    @pl.when(s == 0)
    def _():
        sc_ref[:, pl.ds(0, PAD), :] = jnp.zeros((N_SLAB, PAD, LANE), F32)

        def keep(r0, z):
            _slab_store(sc_ref, PAD + r0, z)
        _row_tiles(h_ref, w_ref, keep, unroll=2)

    @pl.when(s == 1)
    def _():
        def glu_conv(r0, z_ref):
            gate = jax.nn.sigmoid(z_ref[...])
            for j in range(N_SLAB):
                rows = pl.ds(PAD + r0, TR)
                sc_ref[j, rows, :] = sc_ref[j, rows, :] * gate[:, j * LANE:(j + 1) * LANE]
            conv_taps(r0, 0, CONV_SPLIT, cb_ref)
        _pipelined_row_tiles(h_ref, w_ref, z_refs, glu_conv)

    @pl.when(s == 2)
    def _():
        def norm_gate(r0, z):
            rows = pl.ds(r0, TR)
            conv_taps(r0, CONV_SPLIT, CONV_W, cv_ref)
            y = _silu(_layernorm(cv_ref[rows, :], lng_ref[...], lnb_ref[...]))
            o_ref[rows, :] = (y * _silu(z)).astype(BF16)
        _row_tiles(h_ref, w_ref, norm_gate, unroll=2)


def _branch_b(h3, w_in_bf, conv_w, conv_b, ln_g, ln_b):
    b_sz, t_len, _ = h3.shape
    return pl.pallas_call(
        _branch_b_kernel,
        grid=(b_sz, 3),
        in_specs=[pl.BlockSpec((None, t_len, D), lambda b, s: (b, 0, 0)),
                  pl.BlockSpec((D, W), lambda b, s: (0, 3 + s)),
                  pl.BlockSpec((CONV_W, W), lambda b, s: (0, 0)),
                  pl.BlockSpec((1, W), lambda b, s: (0, 0)),
                  pl.BlockSpec((1, W), lambda b, s: (0, 0)),
                  pl.BlockSpec((1, W), lambda b, s: (0, 0))],
        out_specs=pl.BlockSpec((None, t_len, W), lambda b, s: (b, 0, 0)),
        out_shape=jax.ShapeDtypeStruct((b_sz, t_len, W), BF16),
        scratch_shapes=[pltpu.VMEM((N_SLAB, PAD + t_len, LANE), F32),
                        pltpu.VMEM((t_len, W), F32)] + Z_SCRATCH,
        compiler_params=_params(("arbitrary", "arbitrary"), 60),
        name="branch_b",
    )(h3, w_in_bf, conv_w, conv_b.reshape(1, W), ln_g.reshape(1, W), ln_b.reshape(1, W))


def _branch_c_kernel(h_ref, w_ref, pw_ref, scale_ref, o_ref, sc_ref, yc_ref):
    s = pl.program_id(1)

    @pl.when(s == 0)
    def _():
        sc_ref[:, pl.ds(0, PAD), :] = jnp.zeros((N_SLAB, PAD, LANE), F32)

        def pool(r0, z):
            _slab_store(sc_ref, PAD + r0, z)
            for half in range(TR // SHIFT_ROWS):
                t0 = r0 + half * SHIFT_ROWS
                t1 = (t0 + 1 + lax.broadcasted_iota(jnp.int32, (SHIFT_ROWS, POOL_GW), 0)).astype(F32)
                for g, win in enumerate(POOL_WINDOWS):
                    cols = pl.ds(g * POOL_GW, POOL_GW)
                    slabs = range(g * POOL_GW // LANE, (g + 1) * POOL_GW // LANE)

                    def window(shift):
                        return jnp.concatenate(
                            [_shifted_rows(sc_ref, j, PAD + t0 - shift, SHIFT_ROWS) for j in slabs], axis=1)
                    cur = window(0)
                    wsum = cur
                    for k in range(1, win):
                        wsum = wsum + window(k)
                    diff = wsum / jnp.minimum(t1, float(win)) - cur
                    y = jnp.dot(diff.astype(BF16), pw_ref[g].astype(BF16), preferred_element_type=F32)
                    yc_ref[pl.ds(t0, SHIFT_ROWS), cols] = y * scale_ref[:, cols]
        _row_tiles(h_ref, w_ref, pool)

    @pl.when(s == 1)
    def _():
        def silu_gate(r0, z):
            rows = pl.ds(r0, TR)
            o_ref[rows, :] = (yc_ref[rows, :] * _silu(z)).astype(BF16)
        _row_tiles(h_ref, w_ref, silu_gate)


def _branch_c(h3, w_in_bf, pool_w, pool_scale):
    b_sz, t_len, _ = h3.shape
    n_groups = len(POOL_WINDOWS)
    return pl.pallas_call(
        _branch_c_kernel,
        grid=(b_sz, 2),
        in_specs=[pl.BlockSpec((None, t_len, D), lambda b, s: (b, 0, 0)),
                  pl.BlockSpec((D, W), lambda b, s: (0, 6 + s)),
                  pl.BlockSpec((n_groups, POOL_GW, POOL_GW), lambda b, s: (0, 0, 0)),
                  pl.BlockSpec((1, W), lambda b, s: (0, 0))],
        out_specs=pl.BlockSpec((None, t_len, W), lambda b, s: (b, 0, 0)),
        out_shape=jax.ShapeDtypeStruct((b_sz, t_len, W), BF16),
        scratch_shapes=[pltpu.VMEM((N_SLAB, PAD + t_len, LANE), F32),
                        pltpu.VMEM((t_len, W), F32)],
        compiler_params=_params(("arbitrary", "arbitrary"), 56),
        name="branch_c",
    )(h3, w_in_bf, pool_w, pool_scale.reshape(1, W))


def _branch_m_kernel(h_ref, w_ref, kv_ref, o_ref, sc_ref):
    s = pl.program_id(1)

    @pl.when(s == 0)
    def _():
        def attend(r0, z):
            rows = pl.ds(r0, TR)
            q = (z * (HEAD_DIM ** -0.5)).astype(BF16)
            for hd in range(HEADS):
                cols = pl.ds(hd * HEAD_DIM, HEAD_DIM)
                sc = lax.dot_general(q[:, hd * HEAD_DIM:(hd + 1) * HEAD_DIM], kv_ref[:, cols],
                                     (((1,), (1,)), ((), ())), preferred_element_type=F32)
                e = jnp.exp(sc - jnp.max(sc, axis=-1, keepdims=True))
                p = (e / jnp.sum(e, axis=-1, keepdims=True)).astype(BF16)
                sc_ref[rows, cols] = jnp.dot(
                    p, kv_ref[:, pl.ds(W + hd * HEAD_DIM, HEAD_DIM)], preferred_element_type=F32)
        _row_tiles(h_ref, w_ref, attend)

    @pl.when(s == 1)
    def _():
        def silu_gate(r0, z):
            rows = pl.ds(r0, TR)
            o_ref[rows, :] = (sc_ref[rows, :] * _silu(z)).astype(BF16)
        _row_tiles(h_ref, w_ref, silu_gate)


def _branch_m(h3, w_in_bf, kv, l, mem_len):
    b_sz, t_len, _ = h3.shape
    return pl.pallas_call(
        _branch_m_kernel,
        grid=(b_sz, 2),
        in_specs=[pl.BlockSpec((None, t_len, D), lambda b, s: (b, 0, 0)),
                  pl.BlockSpec((D, W), lambda b, s: (0, 8 + s)),
                  pl.BlockSpec((None, mem_len, 2 * W), lambda b, s: (l, b, 0))],
        out_specs=pl.BlockSpec((None, t_len, W), lambda b, s: (b, 0, 0)),
        out_shape=jax.ShapeDtypeStruct((b_sz, t_len, W), BF16),
        scratch_shapes=[pltpu.VMEM((t_len, W), F32)],
        compiler_params=_params(("arbitrary", "arbitrary"), 56),
        name="branch_m",
    )(h3, w_in_bf, kv)


def _merge_kernel(*refs, n_cast):
    h_ref, y_refs, wg_refs, wb_refs = refs[0], refs[1:5], refs[5:9], refs[9:13]
    cast_in, o_ref, cast_out = refs[13:13 + n_cast], refs[13 + n_cast], refs[14 + n_cast:]
    h = h_ref[...]
    acc = None
    for y_ref, wg, wb in zip(y_refs, wg_refs, wb_refs):
        gate = jax.nn.sigmoid(jnp.dot(h, wg[...], preferred_element_type=F32))
        term = gate * jnp.dot(y_ref[...], wb[...], preferred_element_type=F32)
        acc = term if acc is None else acc + term
    o_ref[...] = acc.astype(BF16)
    for src_ref, dst_ref in zip(cast_in, cast_out):
        dst_ref[...] = src_ref[...].astype(BF16)


def _merge(h2, ys, w_in_bf, w_branch_bf, next_weights_f32, l_next, tm=1024, cn=256):
    n = h2.shape[0]
    tm = min(tm, n)
    n_col = D // cn
    n_steps = (n // tm) * n_col
    gate_blk0 = GATE_COL0 // cn

    def wg_spec(nb):
        return pl.BlockSpec((D, cn), lambda i, c: (0, gate_blk0 + nb * n_col + c))

    def wb_spec(nb):
        return pl.BlockSpec((W, cn), lambda i, c: (nb, c))

    cast_in_specs, cast_out_specs, cast_out_shapes = [], [], []
    for w in next_weights_f32:
        rows, cols = w.shape[1:]
        assert rows % (n_steps * BF16_TILE_ROWS) == 0
        cast_in_specs.append(pl.BlockSpec((None, rows // n_steps, cols), lambda i, c: (l_next, i * n_col + c, 0)))
        cast_out_specs.append(pl.BlockSpec((rows // n_steps, cols), lambda i, c: (i * n_col + c, 0)))
        cast_out_shapes.append(jax.ShapeDtypeStruct((rows, cols), BF16))

    y_spec = pl.BlockSpec((tm, W), lambda i, c: (i, 0))
    outs = pl.pallas_call(
        functools.partial(_merge_kernel, n_cast=len(next_weights_f32)),
        grid=(n // tm, n_col),
        in_specs=[pl.BlockSpec((tm, D), lambda i, c: (i, 0))] + [y_spec] * N_BRANCH
                 + [wg_spec(nb) for nb in range(N_BRANCH)]
                 + [wb_spec(nb) for nb in range(N_BRANCH)] + cast_in_specs,
        out_specs=[pl.BlockSpec((tm, cn), lambda i, c: (i, c))] + cast_out_specs,
        out_shape=[jax.ShapeDtypeStruct((n, D), BF16)] + cast_out_shapes,
        compiler_params=_params(("arbitrary", "arbitrary"), 56),
        name="gate_merge",
    )(h2, *ys, *([w_in_bf] * N_BRANCH), *([w_branch_bf] * N_BRANCH), *next_weights_f32)
    return outs[0], tuple(outs[1:])


def _outproj_kernel(x_ref, m_ref, w_ref, g_ref, *out_refs, sub, last):
    def body(r, carry):
        rows = pl.ds(pl.multiple_of(r * sub, sub), sub)
        xn = x_ref[rows, :] + jnp.dot(m_ref[rows, :], w_ref[...], preferred_element_type=F32)
        normed = _rms(xn, g_ref[...])
        if last:
            out_refs[0][rows, :] = normed
        else:
            out_refs[0][rows, :] = xn
            out_refs[1][rows, :] = normed.astype(BF16)
        return carry
    lax.fori_loop(0, x_ref.shape[0] // sub, body, 0, unroll=2)


def _outproj(x2, merged, w_out_bf, g_next, last, tm=512, sub=256):
    n = x2.shape[0]
    tm = min(tm, n)
    row_spec = pl.BlockSpec((tm, D), lambda i: (i, 0))
    if last:
        out_specs, out_shape = row_spec, jax.ShapeDtypeStruct((n, D), F32)
    else:
        out_specs = [row_spec, row_spec]
        out_shape = [jax.ShapeDtypeStruct((n, D), F32), jax.ShapeDtypeStruct((n, D), BF16)]
    return pl.pallas_call(
        functools.partial(_outproj_kernel, sub=min(sub, tm), last=last),
        grid=(n // tm,),
        in_specs=[row_spec, row_spec,
                  pl.BlockSpec((D, D), lambda i: (0, 0)),
                  pl.BlockSpec((1, D), lambda i: (0, 0))],
        out_specs=out_specs,
        out_shape=out_shape,
        compiler_params=_params(("arbitrary",), 56),
        name="out_proj",
    )(x2, merged, w_out_bf, g_next.reshape(1, D))


def kernel(x, mem, norm_g, w_in, gmlp_ln_g, gmlp_ln_b, gmlp_ws, gmlp_bs, conv_w, conv_b,
           conv_ln_g, conv_ln_b, pool_w, pool_scale, mem_norm_g, w_kv, w_branch, w_out, final_g):
    b_sz, t_len, d = x.shape
    n_layers = w_in.shape[0]
    mem_len = mem.shape[1]
    assert d == D and w_in.shape[2] == GATE_COL0 + N_BRANCH * D
    assert t_len % (2 * TR) == 0 and TR % CHUNK == 0

    w_branch2 = w_branch.reshape(n_layers, N_BRANCH * W, D)
    layer_w = (w_in[0].astype(BF16), w_branch2[0].astype(BF16), w_out[0].astype(BF16))
    w_kv_bf = w_kv.astype(BF16)

    n = b_sz * t_len
    x2 = x.reshape(n, D)
    kv = _memory_kv(mem.reshape(b_sz * mem_len, D), mem_norm_g, w_kv_bf)
    h2 = _rmsnorm(x2, norm_g[0], BF16)

    for l in range(n_layers):
        w_in_bf, w_branch_bf, w_out_bf = layer_w
        last = l == n_layers - 1
        h3 = h2.reshape(b_sz, t_len, D)
        bias_full = jnp.repeat(gmlp_bs[l].T, LANE, axis=1)
        ys = (
            _branch_a(h3, w_in_bf, gmlp_ln_g[l], gmlp_ln_b[l], gmlp_ws[l], bias_full),
            _branch_b(h3, w_in_bf, conv_w[l], conv_b[l], conv_ln_g[l], conv_ln_b[l]),
            _branch_c(h3, w_in_bf, pool_w[l], pool_scale[l]),
            _branch_m(h3, w_in_bf, kv, l, mem_len),
        )
        ys = tuple(y.reshape(n, W) for y in ys)
        next_f32 = () if last else (w_in, w_branch2, w_out)
        merged, layer_w = _merge(h2, ys, w_in_bf, w_branch_bf, next_f32, l + 1)
        g_next = final_g if last else norm_g[l + 1]
        res = _outproj(x2, merged, w_out_bf, g_next, last)
        if last:
            return res.reshape(b_sz, t_len, D)
        x2, h2 = res
```

```python
import functools

import jax
import jax.numpy as jnp
from jax import lax
from jax.experimental import pallas as pl
from jax.experimental.pallas import tpu as pltpu

F32 = jnp.float32
BF16 = jnp.bfloat16

D = 2048
W = 1024
CHUNK = 128
G_GMLP = 8
CONV_W = 31
POOL_WINDOWS = (2, 4, 8, 16)
POOL_GW = 256
HEADS = 4
HEAD_DIM = 256
EPS = 1e-6
N_BRANCH = 4
LANE = 128
BF16_TILE_ROWS = 16

PAD = 32
SHIFT_ROWS = 128
CONV_ROWS = 32
CONV_SPLIT = 14
TR = 256
GATE_COL0 = 10 * W

MIB = 1024 * 1024


def _params(sem, vmem_mib):
    return pltpu.CompilerParams(dimension_semantics=sem, vmem_limit_bytes=vmem_mib * MIB)


def _layernorm(x, g, b):
    mu = jnp.mean(x, axis=-1, keepdims=True)
    xc = x - mu
    var = jnp.mean(xc * xc, axis=-1, keepdims=True)
    return xc * lax.rsqrt(var + EPS) * g + b


def _rms(x, g):
    return x * lax.rsqrt(jnp.mean(x * x, axis=-1, keepdims=True) + EPS) * g


def _silu(x):
    return x * jax.nn.sigmoid(x)


def _row_tiles(h_ref, w_ref, epilogue, unroll=None):
    n_tiles = h_ref.shape[0] // TR

    def step(r, carry):
        r0 = r * TR if isinstance(r, int) else pl.multiple_of(r * TR, TR)
        epilogue(r0, jnp.dot(h_ref[pl.ds(r0, TR), :], w_ref[...], preferred_element_type=F32))
        return carry
    if unroll is None:
        for r in range(n_tiles):
            step(r, 0)
    else:
        lax.fori_loop(0, n_tiles, step, 0, unroll=unroll)


def _pipelined_row_tiles(h_ref, w_ref, z_refs, epilogue):
    t_len = h_ref.shape[0]
    n_tiles = t_len // TR
    z0_ref, z1_ref = z_refs

    def project(r0, z_ref):
        z_ref[...] = jnp.dot(h_ref[pl.ds(r0, TR), :], w_ref[...], preferred_element_type=F32)

    project(0, z0_ref)

    def step(i, carry):
        r0 = pl.multiple_of(i * (2 * TR), 2 * TR)
        project(r0 + TR, z1_ref)
        epilogue(r0, z0_ref)
        project(r0 + 2 * TR, z0_ref)
        epilogue(r0 + TR, z1_ref)
        return carry
    lax.fori_loop(0, n_tiles // 2 - 1, step, 0)
    r0 = t_len - 2 * TR
    project(r0 + TR, z1_ref)
    epilogue(r0, z0_ref)
    epilogue(r0 + TR, z1_ref)


Z_SCRATCH = [pltpu.VMEM((TR, W), F32), pltpu.VMEM((TR, W), F32)]
N_SLAB = W // LANE


def _slab_store(ref, start, value):
    for j in range(N_SLAB):
        ref[j, pl.ds(start, value.shape[0]), :] = value[:, j * LANE:(j + 1) * LANE]


def _shifted_rows(ref, slab, start, n_rows):
    return ref[slab, pl.ds(start, n_rows, stride=1), :]


def _rmsnorm_kernel(x_ref, g_ref, o_ref):
    o_ref[...] = _rms(x_ref[...], g_ref[...]).astype(o_ref.dtype)


def _rmsnorm(x2d, g, out_dtype, tm=512):
    n = x2d.shape[0]
    return pl.pallas_call(
        _rmsnorm_kernel,
        grid=(n // tm,),
        in_specs=[pl.BlockSpec((tm, D), lambda i: (i, 0)),
                  pl.BlockSpec((1, D), lambda i: (0, 0))],
        out_specs=pl.BlockSpec((tm, D), lambda i: (i, 0)),
        out_shape=jax.ShapeDtypeStruct((n, D), out_dtype),
        compiler_params=_params(("arbitrary",), 40),
        name="rmsnorm",
    )(x2d, g.reshape(1, D))


def _kv_kernel(m_ref, g_ref, w_ref, o_ref):
    mn = _rms(m_ref[...], g_ref[...]).astype(BF16)
    o_ref[...] = jnp.dot(mn, w_ref[...], preferred_element_type=F32).astype(BF16)


def _memory_kv(mem2d, mem_norm_g, w_kv_bf, tm=512):
    n_layers = w_kv_bf.shape[0]
    n = mem2d.shape[0]
    tm = min(tm, n)
    return pl.pallas_call(
        _kv_kernel,
        grid=(n_layers, n // tm),
        in_specs=[pl.BlockSpec((tm, D), lambda l, i: (i, 0)),
                  pl.BlockSpec((None, 1, D), lambda l, i: (l, 0, 0)),
                  pl.BlockSpec((None, D, 2 * W), lambda l, i: (l, 0, 0))],
        out_specs=pl.BlockSpec((None, tm, 2 * W), lambda l, i: (l, i, 0)),
        out_shape=jax.ShapeDtypeStruct((n_layers, n, 2 * W), BF16),
        compiler_params=_params(("arbitrary", "arbitrary"), 48),
        name="memory_kv",
    )(mem2d, mem_norm_g.reshape(n_layers, 1, D), w_kv_bf)


def _branch_a_kernel(h_ref, w_ref, lng_ref, lnb_ref, ws_ref, bias_ref, o_ref,
                     sc_ref, wsm_ref, vln_ref):
    s = pl.program_id(1)

    @pl.when(s == 0)
    def _():
        row = lax.broadcasted_iota(jnp.int32, (CHUNK, CHUNK), 0)
        col = lax.broadcasted_iota(jnp.int32, (CHUNK, CHUNK), 1)
        for g in range(G_GMLP):
            wsm_ref[g] = jnp.where(row >= col, ws_ref[g], 0.0).astype(BF16)

        def gelu_u(r0, z):
            sc_ref[pl.ds(r0, TR), :] = jax.nn.gelu(z)
        _row_tiles(h_ref, w_ref, gelu_u)

    @pl.when(s == 1)
    def _():
        def spatial_gate(r0, z):
            v = _layernorm(jax.nn.gelu(z), lng_ref[...], lnb_ref[...])
            vln_ref[...] = v.astype(BF16)
            for c in range(TR // CHUNK):
                for g in range(G_GMLP):
                    rows = pl.ds(r0 + c * CHUNK, CHUNK)
                    cols = pl.ds(g * LANE, LANE)
                    sp = jnp.dot(wsm_ref[g], vln_ref[pl.ds(c * CHUNK, CHUNK), cols],
                                 preferred_element_type=F32) + bias_ref[:, cols]
                    sc_ref[rows, cols] = sc_ref[rows, cols] * sp
        _row_tiles(h_ref, w_ref, spatial_gate)

    @pl.when(s == 2)
    def _():
        def silu_gate(r0, z):
            rows = pl.ds(r0, TR)
            o_ref[rows, :] = (sc_ref[rows, :] * _silu(z)).astype(BF16)
        _row_tiles(h_ref, w_ref, silu_gate)


def _branch_a(h3, w_in_bf, ln_g, ln_b, ws, bias_full):
    b_sz, t_len, _ = h3.shape
    return pl.pallas_call(
        _branch_a_kernel,
        grid=(b_sz, 3),
        in_specs=[pl.BlockSpec((None, t_len, D), lambda b, s: (b, 0, 0)),
                  pl.BlockSpec((D, W), lambda b, s: (0, s)),
                  pl.BlockSpec((1, W), lambda b, s: (0, 0)),
                  pl.BlockSpec((1, W), lambda b, s: (0, 0)),
                  pl.BlockSpec((G_GMLP, CHUNK, CHUNK), lambda b, s: (0, 0, 0)),
                  pl.BlockSpec((CHUNK, W), lambda b, s: (0, 0))],
        out_specs=pl.BlockSpec((None, t_len, W), lambda b, s: (b, 0, 0)),
        out_shape=jax.ShapeDtypeStruct((b_sz, t_len, W), BF16),
        scratch_shapes=[pltpu.VMEM((t_len, W), F32),
                        pltpu.VMEM((G_GMLP, CHUNK, CHUNK), BF16),
                        pltpu.VMEM((TR, W), BF16)],
        compiler_params=_params(("arbitrary", "arbitrary"), 56),
        name="branch_a",
    )(h3, w_in_bf, ln_g.reshape(1, W), ln_b.reshape(1, W), ws, bias_full)


def _branch_b_kernel(h_ref, w_ref, cw_ref, cb_ref, lng_ref, lnb_ref, o_ref,
                     sc_ref, cv_ref, z0_ref, z1_ref):
    s = pl.program_id(1)
    z_refs = (z0_ref, z1_ref)

    def conv_taps(r0, k_lo, k_hi, init_ref):
        for j in range(N_SLAB):
            cols = pl.ds(j * LANE, LANE)
            for part in range(TR // CONV_ROWS):
                rows = pl.ds(r0 + part * CONV_ROWS, CONV_ROWS)
                base = r0 + part * CONV_ROWS + PAD - (CONV_W - 1)
                acc = init_ref[rows, cols] if init_ref is cv_ref else init_ref[:, cols]
                for k in range(k_lo, k_hi):
                    acc = acc + _shifted_rows(sc_ref, j, base + k, CONV_ROWS) * cw_ref[k:k + 1, cols]
                cv_ref[rows, cols] = acc

    @pl.when(s == 0)
    def _():
        sc_ref[:, pl.ds(0, PAD), :] = jnp.zeros((N_SLAB, PAD, LANE), F32)

        def keep(r0, z):
            _slab_store(sc_ref, PAD + r0, z)
        _row_tiles(h_ref, w_ref, keep, unroll=2)

    @pl.when(s == 1)
    def _():
        def glu_conv(r0, z_ref):
            gate = jax.nn.sigmoid(z_ref[...])
            for j in range(N_SLAB):
                rows = pl.ds(PAD + r0, TR)
                sc_ref[j, rows, :] = sc_ref[j, rows, :] * gate[:, j * LANE:(j + 1) * LANE]
            conv_taps(r0, 0, CONV_SPLIT, cb_ref)
        _pipelined_row_tiles(h_ref, w_ref, z_refs, glu_conv)

    @pl.when(s == 2)
    def _():
        def norm_gate(r0, z):
            rows = pl.ds(r0, TR)
            conv_taps(r0, CONV_SPLIT, CONV_W, cv_ref)
            y = _silu(_layernorm(cv_ref[rows, :], lng_ref[...], lnb_ref[...]))
            o_ref[rows, :] = (y * _silu(z)).astype(BF16)
        _row_tiles(h_ref, w_ref, norm_gate, unroll=2)


def _branch_b(h3, w_in_bf, conv_w, conv_b, ln_g, ln_b):
    b_sz, t_len, _ = h3.shape
    return pl.pallas_call(
        _branch_b_kernel,
        grid=(b_sz, 3),
        in_specs=[pl.BlockSpec((None, t_len, D), lambda b, s: (b, 0, 0)),
                  pl.BlockSpec((D, W), lambda b, s: (0, 3 + s)),
                  pl.BlockSpec((CONV_W, W), lambda b, s: (0, 0)),
                  pl.BlockSpec((1, W), lambda b, s: (0, 0)),
                  pl.BlockSpec((1, W), lambda b, s: (0, 0)),
                  pl.BlockSpec((1, W), lambda b, s: (0, 0))],
        out_specs=pl.BlockSpec((None, t_len, W), lambda b, s: (b, 0, 0)),
        out_shape=jax.ShapeDtypeStruct((b_sz, t_len, W), BF16),
        scratch_shapes=[pltpu.VMEM((N_SLAB, PAD + t_len, LANE), F32),
                        pltpu.VMEM((t_len, W), F32)] + Z_SCRATCH,
        compiler_params=_params(("arbitrary", "arbitrary"), 60),
        name="branch_b",
    )(h3, w_in_bf, conv_w, conv_b.reshape(1, W), ln_g.reshape(1, W), ln_b.reshape(1, W))


def _branch_c_kernel(h_ref, w_ref, pw_ref, scale_ref, o_ref, sc_ref, yc_ref):
    s = pl.program_id(1)

    @pl.when(s == 0)
    def _():
        sc_ref[:, pl.ds(0, PAD), :] = jnp.zeros((N_SLAB, PAD, LANE), F32)

        def pool(r0, z):
            _slab_store(sc_ref, PAD + r0, z)
            for half in range(TR // SHIFT_ROWS):
                t0 = r0 + half * SHIFT_ROWS
                t1 = (t0 + 1 + lax.broadcasted_iota(jnp.int32, (SHIFT_ROWS, POOL_GW), 0)).astype(F32)
                for g, win in enumerate(POOL_WINDOWS):
                    cols = pl.ds(g * POOL_GW, POOL_GW)
                    slabs = range(g * POOL_GW // LANE, (g + 1) * POOL_GW // LANE)

                    def window(shift):
                        return jnp.concatenate(
                            [_shifted_rows(sc_ref, j, PAD + t0 - shift, SHIFT_ROWS) for j in slabs], axis=1)
                    cur = window(0)
                    wsum = cur
                    for k in range(1, win):
                        wsum = wsum + window(k)
                    diff = wsum / jnp.minimum(t1, float(win)) - cur
                    y = jnp.dot(diff.astype(BF16), pw_ref[g].astype(BF16), preferred_element_type=F32)
                    yc_ref[pl.ds(t0, SHIFT_ROWS), cols] = y * scale_ref[:, cols]
        _row_tiles(h_ref, w_ref, pool)

    @pl.when(s == 1)
    def _():
        def silu_gate(r0, z):
            rows = pl.ds(r0, TR)
            o_ref[rows, :] = (yc_ref[rows, :] * _silu(z)).astype(BF16)
        _row_tiles(h_ref, w_ref, silu_gate)


def _branch_c(h3, w_in_bf, pool_w, pool_scale):
    b_sz, t_len, _ = h3.shape
    n_groups = len(POOL_WINDOWS)
    return pl.pallas_call(
        _branch_c_kernel,
        grid=(b_sz, 2),
        in_specs=[pl.BlockSpec((None, t_len, D), lambda b, s: (b, 0, 0)),
                  pl.BlockSpec((D, W), lambda b, s: (0, 6 + s)),
                  pl.BlockSpec((n_groups, POOL_GW, POOL_GW), lambda b, s: (0, 0, 0)),
                  pl.BlockSpec((1, W), lambda b, s: (0, 0))],
        out_specs=pl.BlockSpec((None, t_len, W), lambda b, s: (b, 0, 0)),
        out_shape=jax.ShapeDtypeStruct((b_sz, t_len, W), BF16),
        scratch_shapes=[pltpu.VMEM((N_SLAB, PAD + t_len, LANE), F32),
                        pltpu.VMEM((t_len, W), F32)],
        compiler_params=_params(("arbitrary", "arbitrary"), 56),
        name="branch_c",
    )(h3, w_in_bf, pool_w, pool_scale.reshape(1, W))


def _branch_m_kernel(h_ref, w_ref, kv_ref, o_ref, sc_ref):
    s = pl.program_id(1)

    @pl.when(s == 0)
    def _():
        def attend(r0, z):
            rows = pl.ds(r0, TR)
            q = (z * (HEAD_DIM ** -0.5)).astype(BF16)
            for hd in range(HEADS):
                cols = pl.ds(hd * HEAD_DIM, HEAD_DIM)
                sc = lax.dot_general(q[:, hd * HEAD_DIM:(hd + 1) * HEAD_DIM], kv_ref[:, cols],
                                     (((1,), (1,)), ((), ())), preferred_element_type=F32)
                e = jnp.exp(sc - jnp.max(sc, axis=-1, keepdims=True))
                p = (e / jnp.sum(e, axis=-1, keepdims=True)).astype(BF16)
                sc_ref[rows, cols] = jnp.dot(
                    p, kv_ref[:, pl.ds(W + hd * HEAD_DIM, HEAD_DIM)], preferred_element_type=F32)
        _row_tiles(h_ref, w_ref, attend)

    @pl.when(s == 1)
    def _():
        def silu_gate(r0, z):
            rows = pl.ds(r0, TR)
            o_ref[rows, :] = (sc_ref[rows, :] * _silu(z)).astype(BF16)
        _row_tiles(h_ref, w_ref, silu_gate)


def _branch_m(h3, w_in_bf, kv, l, mem_len):
    b_sz, t_len, _ = h3.shape
    return pl.pallas_call(
        _branch_m_kernel,
        grid=(b_sz, 2),
        in_specs=[pl.BlockSpec((None, t_len, D), lambda b, s: (b, 0, 0)),
                  pl.BlockSpec((D, W), lambda b, s: (0, 8 + s)),
                  pl.BlockSpec((None, mem_len, 2 * W), lambda b, s: (l, b, 0))],
        out_specs=pl.BlockSpec((None, t_len, W), lambda b, s: (b, 0, 0)),
        out_shape=jax.ShapeDtypeStruct((b_sz, t_len, W), BF16),
        scratch_shapes=[pltpu.VMEM((t_len, W), F32)],
        compiler_params=_params(("arbitrary", "arbitrary"), 56),
        name="branch_m",
    )(h3, w_in_bf, kv)


def _merge_kernel(*refs, n_cast):
    h_ref, y_refs, wg_refs, wb_refs = refs[0], refs[1:5], refs[5:9], refs[9:13]
    cast_in, o_ref, cast_out = refs[13:13 + n_cast], refs[13 + n_cast], refs[14 + n_cast:]
    h = h_ref[...]
    acc = None
    for y_ref, wg, wb in zip(y_refs, wg_refs, wb_refs):
        gate = jax.nn.sigmoid(jnp.dot(h, wg[...], preferred_element_type=F32))
        term = gate * jnp.dot(y_ref[...], wb[...], preferred_element_type=F32)
        acc = term if acc is None else acc + term
    o_ref[...] = acc.astype(BF16)
    for src_ref, dst_ref in zip(cast_in, cast_out):
        dst_ref[...] = src_ref[...].astype(BF16)


def _merge(h2, ys, w_in_bf, w_branch_bf, next_weights_f32, l_next, tm=1024, cn=256):
    n = h2.shape[0]
    tm = min(tm, n)
    n_col = D // cn
    n_steps = (n // tm) * n_col
    gate_blk0 = GATE_COL0 // cn

    def wg_spec(nb):
        return pl.BlockSpec((D, cn), lambda i, c: (0, gate_blk0 + nb * n_col + c))

    def wb_spec(nb):
        return pl.BlockSpec((W, cn), lambda i, c: (nb, c))

    cast_in_specs, cast_out_specs, cast_out_shapes = [], [], []
    for w in next_weights_f32:
        rows, cols = w.shape[1:]
        assert rows % (n_steps * BF16_TILE_ROWS) == 0
        cast_in_specs.append(pl.BlockSpec((None, rows // n_steps, cols), lambda i, c: (l_next, i * n_col + c, 0)))
        cast_out_specs.append(pl.BlockSpec((rows // n_steps, cols), lambda i, c: (i * n_col + c, 0)))
        cast_out_shapes.append(jax.ShapeDtypeStruct((rows, cols), BF16))

    y_spec = pl.BlockSpec((tm, W), lambda i, c: (i, 0))
    outs = pl.pallas_call(
        functools.partial(_merge_kernel, n_cast=len(next_weights_f32)),
        grid=(n // tm, n_col),
        in_specs=[pl.BlockSpec((tm, D), lambda i, c: (i, 0))] + [y_spec] * N_BRANCH
                 + [wg_spec(nb) for nb in range(N_BRANCH)]
                 + [wb_spec(nb) for nb in range(N_BRANCH)] + cast_in_specs,
        out_specs=[pl.BlockSpec((tm, cn), lambda i, c: (i, c))] + cast_out_specs,
        out_shape=[jax.ShapeDtypeStruct((n, D), BF16)] + cast_out_shapes,
        compiler_params=_params(("arbitrary", "arbitrary"), 56),
        name="gate_merge",
    )(h2, *ys, *([w_in_bf] * N_BRANCH), *([w_branch_bf] * N_BRANCH), *next_weights_f32)
    return outs[0], tuple(outs[1:])


def _outproj_kernel(x_ref, m_ref, w_ref, g_ref, *out_refs, sub, last):
    def body(r, carry):
        rows = pl.ds(pl.multiple_of(r * sub, sub), sub)
        xn = x_ref[rows, :] + jnp.dot(m_ref[rows, :], w_ref[...], preferred_element_type=F32)
        normed = _rms(xn, g_ref[...])
        if last:
            out_refs[0][rows, :] = normed
        else:
            out_refs[0][rows, :] = xn
            out_refs[1][rows, :] = normed.astype(BF16)
        return carry
    lax.fori_loop(0, x_ref.shape[0] // sub, body, 0, unroll=2)


def _outproj(x2, merged, w_out_bf, g_next, last, tm=512, sub=256):
    n = x2.shape[0]
    tm = min(tm, n)
    row_spec = pl.BlockSpec((tm, D), lambda i: (i, 0))
    if last:
        out_specs, out_shape = row_spec, jax.ShapeDtypeStruct((n, D), F32)
    else:
        out_specs = [row_spec, row_spec]
        out_shape = [jax.ShapeDtypeStruct((n, D), F32), jax.ShapeDtypeStruct((n, D), BF16)]
    return pl.pallas_call(
        functools.partial(_outproj_kernel, sub=min(sub, tm), last=last),
        grid=(n // tm,),
        in_specs=[row_spec, row_spec,
                  pl.BlockSpec((D, D), lambda i: (0, 0)),
                  pl.BlockSpec((1, D), lambda i: (0, 0))],
        out_specs=out_specs,
        out_shape=out_shape,
        compiler_params=_params(("arbitrary",), 56),
        name="out_proj",
    )(x2, merged, w_out_bf, g_next.reshape(1, D))


def kernel(x, mem, norm_g, w_in, gmlp_ln_g, gmlp_ln_b, gmlp_ws, gmlp_bs, conv_w, conv_b,
           conv_ln_g, conv_ln_b, pool_w, pool_scale, mem_norm_g, w_kv, w_branch, w_out, final_g):
    b_sz, t_len, d = x.shape
    n_layers = w_in.shape[0]
    mem_len = mem.shape[1]
    assert d == D and w_in.shape[2] == GATE_COL0 + N_BRANCH * D
    assert t_len % (2 * TR) == 0 and TR % CHUNK == 0

    w_branch2 = w_branch.reshape(n_layers, N_BRANCH * W, D)
    layer_w = (w_in[0].astype(BF16), w_branch2[0].astype(BF16), w_out[0].astype(BF16))
    w_kv_bf = w_kv.astype(BF16)

    n = b_sz * t_len
    x2 = x.reshape(n, D)
    kv = _memory_kv(mem.reshape(b_sz * mem_len, D), mem_norm_g, w_kv_bf)
    h2 = _rmsnorm(x2, norm_g[0], BF16)

    for l in range(n_layers):
        w_in_bf, w_branch_bf, w_out_bf = layer_w
        last = l == n_layers - 1
        h3 = h2.reshape(b_sz, t_len, D)
        bias_full = jnp.repeat(gmlp_bs[l].T, LANE, axis=1)
        ys = (
            _branch_a(h3, w_in_bf, gmlp_ln_g[l], gmlp_ln_b[l], gmlp_ws[l], bias_full),
            _branch_b(h3, w_in_bf, conv_w[l], conv_b[l], conv_ln_g[l], conv_ln_b[l]),
            _branch_c(h3, w_in_bf, pool_w[l], pool_scale[l]),
            _branch_m(h3, w_in_bf, kv, l, mem_len),
        )
        ys = tuple(y.reshape(n, W) for y in ys)
        next_f32 = () if last else (w_in, w_branch2, w_out)
        merged, layer_w = _merge(h2, ys, w_in_bf, w_branch_bf, next_f32, l + 1)
        g_next = final_g if last else norm_g[l + 1]
        res = _outproj(x2, merged, w_out_bf, g_next, last)
        if last:
            return res.reshape(b_sz, t_len, D)
        x2, h2 = res
```
